```python
import math
import jax, jax.numpy as jnp
from jax import lax
import numpy as np

D_MODEL = 4096
BATCH = 2
SEQ = 4096
DEPTH = 4

GRID_W = 64
CTX_LEN = 256
HEAD_DIM = 128
MIX_HEADS = D_MODEL // HEAD_DIM
MIX_W = MIX_HEADS * HEAD_DIM
ADA_RANK = 256
EPS = 1e-6
ROPE_THETA = 10000.0
A_HEADS = MIX_HEADS // 2
B_GROUPS = MIX_HEADS - A_HEADS
A_W = A_HEADS * HEAD_DIM
B_W = B_GROUPS * HEAD_DIM
CHUNK = 128
EVEN_IN = 2 * A_W + B_W
C_HEADS = MIX_HEADS // 2
D_HEADS = (MIX_HEADS - C_HEADS) // 2
C_W = C_HEADS * HEAD_DIM
D_W = D_HEADS * 2 * HEAD_DIM
ODD_IN = 3 * C_W + 3 * D_W
NA_KH = 8
NA_KW = 16
NA_QR = 8
NA_QC = 16
Q_BLOCK = 128
ATTN_SCALE = HEAD_DIM ** -0.5
NEG = -1e30
N_EXPERTS = 32
TOP_K = 4
D_EXPERT = 384
SWIGLU_LIMIT = 7.0
SWIGLU_ALPHA = 1.702
N_EVEN = (DEPTH + 1) // 2
N_ODD = DEPTH // 2

kernel_name = 'hybrid_diffusion_backbone'

F32 = jnp.float32


def rmsnorm(x, g):
    xf = x.astype(F32)
    y = xf * lax.rsqrt(jnp.mean(xf * xf, axis=-1, keepdims=True) + EPS)
    return (y * g.astype(F32)).astype(x.dtype)


def modulate(x, g, shift, scale):
    return rmsnorm(x, g) * (1 + scale) + shift


def ada_mod(cvec, w1, w2, b):
    m = jax.nn.silu(cvec) @ w1 @ w2 + b
    return jnp.split(m[:, None, :], 6, axis=-1)


def axial_rope(n):
    pos = jnp.arange(n)
    row = (pos // GRID_W).astype(F32)
    col = (pos % GRID_W).astype(F32)
    nf = HEAD_DIM // 4
    inv = ROPE_THETA ** (-jnp.arange(nf, dtype=F32) / nf)
    ang = jnp.concatenate([row[:, None] * inv, col[:, None] * inv], axis=-1)
    return jnp.cos(ang), jnp.sin(ang)


def apply_rope(t, cos, sin):
    shp = (1, cos.shape[0]) + (1,) * (t.ndim - 3) + (cos.shape[1],)
    cs, sn = cos.reshape(shp), sin.reshape(shp)
    t1, t2 = jnp.split(t.astype(F32), 2, axis=-1)
    return jnp.concatenate([t1 * cs - t2 * sn, t1 * sn + t2 * cs], axis=-1).astype(t.dtype)


def even_mixer(h, w_in, g_v, w_s, b_s, w_out):
    b, n, _ = h.shape
    p = h @ w_in
    pa = jax.nn.gelu(p[..., :2 * A_W])
    u = pa[..., :A_W].reshape(b, n, A_HEADS, HEAD_DIM)
    v = rmsnorm(pa[..., A_W:].reshape(b, n, A_HEADS, HEAD_DIM), g_v.reshape(A_HEADS, HEAD_DIM))
    v = v.reshape(b, n // CHUNK, CHUNK, A_HEADS, HEAD_DIM)
    mixed = jnp.einsum('hpq,bcqhd->bcphd', w_s, v) + b_s.T[None, None, :, :, None]
    y_a = (u * mixed.reshape(b, n, A_HEADS, HEAD_DIM)).reshape(b, n, A_W)
    zb = p[..., 2 * A_W:].reshape(b, n, B_GROUPS, HEAD_DIM)
    y_b = jnp.real(jnp.fft.fft2(zb.astype(F32), axes=(1, 3), norm='ortho')).astype(h.dtype)
    return jnp.concatenate([y_a, y_b.reshape(b, n, B_W)], axis=-1) @ w_out


def neighbourhood_attention(q, k, v, k_ctx, v_ctx, rpb):
    b, l, h, d = q.shape
    rows = l // GRID_W
    kh = min(NA_KH, rows)
    qr = math.gcd(rows, NA_QR)
    kb = min(kh + qr - 1, rows)
    kbw = min(NA_KW + NA_QC, GRID_W)
    n_rb, n_cb = rows // qr, GRID_W // NA_QC
    q_row = jnp.arange(rows).reshape(n_rb, qr)
    q_col = jnp.arange(GRID_W).reshape(n_cb, NA_QC)
    band_r = jnp.clip(q_row[:, 0] - kh // 2, 0, rows - kb)
    band_c = jnp.clip(q_col[:, 0] - NA_KW // 2, 0, GRID_W - kbw)
    ridx = band_r[:, None] + jnp.arange(kb)
    cidx = band_c[:, None] + jnp.arange(kbw)
    win_r = jnp.clip(q_row - kh // 2, 0, rows - kh)[..., None]
    win_c = jnp.clip(q_col - NA_KW // 2, 0, GRID_W - NA_KW)[..., None]
    key_r, key_c = ridx[:, None, :], cidx[:, None, :]
    ok_r = (key_r >= win_r) & (key_r < win_r + kh)
    ok_c = (key_c >= win_c) & (key_c < win_c + NA_KW)
    dr = jnp.clip(key_r - q_row[..., None], 1 - NA_KH, NA_KH - 1) + NA_KH - 1
    dc = jnp.clip(key_c - q_col[..., None], 1 - NA_KW, NA_KW - 1) + NA_KW - 1
    bias = rpb[:, dr[:, :, None, None, :, None], dc[None, None, :, :, None, :]]
    mask = ok_r[:, :, None, None, :, None] & ok_c[None, None, :, :, None, :]
    bias = jnp.where(mask, bias.astype(F32), NEG)
    qg = q.reshape(b, n_rb, qr, n_cb, NA_QC, h, d)

    def band(t):
        t = jnp.take(t.reshape(b, rows, GRID_W, h, d), ridx, axis=1)
        return jnp.take(t, cidx, axis=3)

    k_band, v_band = band(k), band(v)
    s_loc = jnp.einsum('bixjyhd,biujwhd->bhixjyuw', qg, k_band).astype(F32) * ATTN_SCALE + bias[None]
    s_ctx = jnp.einsum('bixjyhd,bchd->bhixjyc', qg, k_ctx).astype(F32) * ATTN_SCALE
    n_loc = kb * kbw
    s = jnp.concatenate([s_loc.reshape(s_loc.shape[:6] + (n_loc,)), s_ctx], axis=-1)
    p = jax.nn.softmax(s, axis=-1).astype(v.dtype)
    p_loc = p[..., :n_loc].reshape(s_loc.shape)
    p_ctx = p[..., n_loc:]
    o = (jnp.einsum('bhixjyuw,biujwhd->bixjyhd', p_loc, v_band)
         + jnp.einsum('bhixjyc,bchd->bixjyhd', p_ctx, v_ctx))
    return o.reshape(b, l, h * d)


def dense_attention(q, k, v):
    s = jnp.einsum('bqhd,bkhd->bhqk', q, k).astype(F32) * ATTN_SCALE
    p = jax.nn.softmax(s, axis=-1).astype(v.dtype)
    return jnp.einsum('bhqk,bkhd->bqhd', p, v)


def diff_block(qi, k, v, lam):
    s = jnp.einsum('bqhtd,bkhtd->bthqk', qi, k).astype(F32) * ATTN_SCALE
    p = jax.nn.softmax(s, axis=-1)
    w = p[:, 0] - lam * p[:, 1]
    return jnp.einsum('bhqk,bkhe->bqhe', w.astype(v.dtype), v)


def odd_mixer(h_lat, h_ctx, w_in, rpb, lam_p, g_sub, w_out, lambda_init, need_ctx):
    b, l, _ = h_lat.shape

    def project(hh):
        bb, n, _ = hh.shape
        p = hh @ w_in
        qc = p[..., :C_W].reshape(bb, n, C_HEADS, HEAD_DIM)
        kc = p[..., C_W:2 * C_W].reshape(bb, n, C_HEADS, HEAD_DIM)
        vc = p[..., 2 * C_W:3 * C_W].reshape(bb, n, C_HEADS, HEAD_DIM)
        o = 3 * C_W
        qd = p[..., o:o + D_W].reshape(bb, n, D_HEADS, 2, HEAD_DIM)
        kd = p[..., o + D_W:o + 2 * D_W].reshape(bb, n, D_HEADS, 2, HEAD_DIM)
        vd = p[..., o + 2 * D_W:].reshape(bb, n, D_HEADS, 2 * HEAD_DIM)
        return qc, kc, vc, qd, kd, vd

    qc_l, kc_l, vc_l, qd_l, kd_l, vd_l = project(h_lat)
    qc_x, kc_x, vc_x, qd_x, kd_x, vd_x = project(h_ctx)
    lp = lam_p.astype(F32)
    lam = jnp.exp(jnp.sum(lp[0] * lp[1])) - jnp.exp(jnp.sum(lp[2] * lp[3])) + lambda_init
    cos, sin = axial_rope(l)
    qd_l = apply_rope(qd_l, cos, sin)
    kd_l = apply_rope(kd_l, cos, sin)
    kd_all = jnp.concatenate([kd_x, kd_l], axis=1)
    vd_all = jnp.concatenate([vd_x, vd_l], axis=1)
    n_blk = l // Q_BLOCK
    qb = jnp.moveaxis(qd_l.reshape(b, n_blk, Q_BLOCK, D_HEADS, 2, HEAD_DIM), 1, 0)
    od_l = lax.map(lambda qi: diff_block(qi, kd_all, vd_all, lam), qb)
    od_l = jnp.moveaxis(od_l, 0, 1).reshape(b, l, D_HEADS, 2 * HEAD_DIM)

    def finish(o_c, o_d):
        o_d = rmsnorm(o_d, g_sub) * (1 - lambda_init)
        return jnp.concatenate([o_c, o_d.reshape(o_d.shape[0], o_d.shape[1], D_W)], axis=-1) @ w_out

    y_lat = finish(neighbourhood_attention(qc_l, kc_l, vc_l, kc_x, vc_x, rpb), od_l)
    y_ctx = None
    if need_ctx:
        n_ctx = h_ctx.shape[1]
        o_cx = dense_attention(qc_x, kc_x, vc_x).reshape(b, n_ctx, C_W)
        o_dx = diff_block(qd_x, kd_x, vd_x, lam)
        y_ctx = finish(o_cx, o_dx)
    return y_lat, y_ctx


def moe(h, w_r, b_r, w_gu, b_gu, w_dn, b_dn):
    shp = h.shape
    t = h.reshape(-1, shp[-1])
    logits = (t @ w_r + b_r).astype(F32)
    top_v, top_i = lax.top_k(logits, TOP_K)
    wts = jax.nn.softmax(top_v, axis=-1)
    gates = jnp.sum(jax.nn.one_hot(top_i, N_EXPERTS, dtype=F32) * wts[..., None], axis=1)
    gu = jnp.einsum('td,edf->etf', t, w_gu) + b_gu[:, None, :]
    gate, up = jnp.split(gu, 2, axis=-1)
    gate = jnp.minimum(gate, SWIGLU_LIMIT)
    up = jnp.clip(up, -SWIGLU_LIMIT, SWIGLU_LIMIT)
    act = gate * jax.nn.sigmoid(SWIGLU_ALPHA * gate) * (up + 1)
    act = act * gates.T[:, :, None].astype(act.dtype)
    out = jnp.einsum('etf,efd->td', act, w_dn) + gates.astype(t.dtype) @ b_dn
    return out.reshape(shp)


def setup_inputs(seed: int = 0) -> dict:
    key = jax.random.key(seed)
    ks = jax.random.split(key, 26)
    D = D_MODEL

    def nrm(k, shape, scale):
        return jax.random.normal(k, shape, F32) * scale

    return {
        'x': nrm(ks[0], (BATCH, SEQ, D), 1.0),
        'c': nrm(ks[1], (BATCH, D), 1.0),
        'ctx': nrm(ks[2], (BATCH, CTX_LEN, D), 1.0),
        'c_ctx': nrm(ks[3], (D,), 1.0),
        'ada_w1': nrm(ks[4], (DEPTH, D, ADA_RANK), D ** -0.5),
        'ada_w2': nrm(ks[5], (DEPTH, ADA_RANK, 6 * D), 0.5 * ADA_RANK ** -0.5),
        'ada_b': nrm(ks[6], (DEPTH, 6 * D), 0.02),
        'norm_mix': 1 + nrm(ks[7], (DEPTH, D), 0.02),
        'norm_ffn': 1 + nrm(ks[8], (DEPTH, D), 0.02),
        'ev_w_in': nrm(ks[9], (N_EVEN, D, EVEN_IN), D ** -0.5),
        'ev_v_norm': 1 + nrm(ks[10], (N_EVEN, A_W), 0.02),
        'ev_w_s': nrm(ks[11], (N_EVEN, A_HEADS, CHUNK, CHUNK), CHUNK ** -0.5),
        'ev_b_s': 1 + nrm(ks[12], (N_EVEN, A_HEADS, CHUNK), 0.02),
        'ev_w_out': nrm(ks[13], (N_EVEN, MIX_W, D), MIX_W ** -0.5),
        'od_w_in': nrm(ks[14], (N_ODD, D, ODD_IN), D ** -0.5),
        'od_rpb': nrm(ks[15], (N_ODD, C_HEADS, 2 * NA_KH - 1, 2 * NA_KW - 1), 0.02),
        'od_lam': nrm(ks[16], (N_ODD, 4, HEAD_DIM), 0.1),
        'od_sub_norm': 1 + nrm(ks[17], (N_ODD, 2 * HEAD_DIM), 0.02),
        'od_w_out': nrm(ks[18], (N_ODD, MIX_W, D), MIX_W ** -0.5),
        'moe_w_r': nrm(ks[19], (DEPTH, D, N_EXPERTS), D ** -0.5),
        'moe_b_r': nrm(ks[20], (DEPTH, N_EXPERTS), 0.01),
        'moe_w_gu': nrm(ks[21], (DEPTH, N_EXPERTS, D, 2 * D_EXPERT), D ** -0.5),
        'moe_b_gu': nrm(ks[22], (DEPTH, N_EXPERTS, 2 * D_EXPERT), 0.02),
        'moe_w_dn': nrm(ks[23], (DEPTH, N_EXPERTS, D_EXPERT, D), D_EXPERT ** -0.5),
        'moe_b_dn': nrm(ks[24], (DEPTH, N_EXPERTS, D), 0.02),
        'norm_final': 1 + nrm(ks[25], (D,), 0.02),
    }


def reference(x, c, ctx, c_ctx, ada_w1, ada_w2, ada_b, norm_mix, norm_ffn,
              ev_w_in, ev_v_norm, ev_w_s, ev_b_s, ev_w_out,
              od_w_in, od_rpb, od_lam, od_sub_norm, od_w_out,
              moe_w_r, moe_b_r, moe_w_gu, moe_b_gu, moe_w_dn, moe_b_dn, norm_final):
    x_lat, x_ctx = x, ctx
    for i in range(DEPTH):
        last = i == DEPTH - 1
        j = i // 2
        sh1, sc1, g1, sh2, sc2, g2 = ada_mod(c, ada_w1[i], ada_w2[i], ada_b[i])
        csh1, csc1, cg1, csh2, csc2, cg2 = ada_mod(c_ctx[None, :], ada_w1[i], ada_w2[i], ada_b[i])
        h_lat = modulate(x_lat, norm_mix[i], sh1, sc1)
        y_ctx = None
        if i % 2 == 0:
            y_lat = even_mixer(h_lat, ev_w_in[j], ev_v_norm[j], ev_w_s[j], ev_b_s[j], ev_w_out[j])
            if not last:
                h_ctx = modulate(x_ctx, norm_mix[i], csh1, csc1)
                y_ctx = even_mixer(h_ctx, ev_w_in[j], ev_v_norm[j], ev_w_s[j], ev_b_s[j], ev_w_out[j])
        else:
            h_ctx = modulate(x_ctx, norm_mix[i], csh1, csc1)
            lambda_init = 0.8 - 0.6 * math.exp(-0.3 * i)
            y_lat, y_ctx = odd_mixer(h_lat, h_ctx, od_w_in[j], od_rpb[j], od_lam[j], od_sub_norm[j],
                                     od_w_out[j], lambda_init, not last)
        x_lat = x_lat + g1 * y_lat
        x_lat = x_lat + g2 * moe(modulate(x_lat, norm_ffn[i], sh2, sc2), moe_w_r[i], moe_b_r[i],
                                 moe_w_gu[i], moe_b_gu[i], moe_w_dn[i], moe_b_dn[i])
        if not last:
            x_ctx = x_ctx + cg1 * y_ctx
            x_ctx = x_ctx + cg2 * moe(modulate(x_ctx, norm_ffn[i], csh2, csc2), moe_w_r[i], moe_b_r[i],
                                      moe_w_gu[i], moe_b_gu[i], moe_w_dn[i], moe_b_dn[i])
    return rmsnorm(x_lat, norm_final)
```

```python
import functools
import math

import numpy as np
import jax
import jax.numpy as jnp
from jax import lax
from jax.experimental import pallas as pl
from jax.experimental.pallas import tpu as pltpu

F32 = jnp.float32
BF16 = jnp.bfloat16

GRID_W = 64
HEAD_DIM = 128
EPS = 1e-6
ROPE_THETA = 10000.0
CHUNK = 128
NA_KH = 8
NA_KW = 16
NA_QR = 8
ATTN_SCALE = HEAD_DIM ** -0.5
NEG = -1e30
N_EXPERTS = 32
TOP_K = 4
SWIGLU_LIMIT = 7.0
SWIGLU_ALPHA = 1.702
LANES = 128

VMEM_LIMIT = 56 * 1024 * 1024


def _cp(sem, vmem=VMEM_LIMIT):
    return pltpu.CompilerParams(dimension_semantics=sem, vmem_limit_bytes=vmem)


def _gelu_tanh(x):
    return 0.5 * x * (1.0 + jnp.tanh(0.7978845608028654 * (x + 0.044715 * (x * x * x))))


def _sigmoid(x):
    return 1.0 / (1.0 + jnp.exp(-x))


def _ada_kernel(cv_ref, w1_ref, w2_ref, b_ref, o_ref):
    cv = cv_ref[...]
    s = cv * _sigmoid(cv)
    t = jnp.dot(s, w1_ref[0], preferred_element_type=F32, precision=lax.Precision.HIGHEST)
    m = jnp.dot(t, w2_ref[0], preferred_element_type=F32, precision=lax.Precision.HIGHEST)
    o_ref[0] = m + b_ref[0]


def ada_all(cv, w1, w2, b):
    nl, d, r = w1.shape
    n6 = w2.shape[2]
    tn = 2048
    return pl.pallas_call(
        _ada_kernel,
        grid=(nl, n6 // tn),
        in_specs=[
            pl.BlockSpec((8, d), lambda l, j: (0, 0)),
            pl.BlockSpec((1, d, r), lambda l, j: (l, 0, 0)),
            pl.BlockSpec((1, r, tn), lambda l, j: (l, 0, j)),
            pl.BlockSpec((1, 1, tn), lambda l, j: (l, 0, j)),
        ],
        out_specs=pl.BlockSpec((1, 8, tn), lambda l, j: (l, 0, j)),
        out_shape=jax.ShapeDtypeStruct((nl, 8, n6), F32),
        compiler_params=_cp(("arbitrary", "arbitrary")),
        name="ada_mod",
    )(cv, w1, w2, b.reshape(nl, 1, n6))


def _seg_fn(n_lat_tiles, tiles_per_batch, n_batch):
    def seg(i):
        return jnp.where(i < n_lat_tiles, i // tiles_per_batch, n_batch)
    return seg


def _mod_kernel(x_ref, g_ref, sc_ref, sh_ref, o_ref):
    x = x_ref[...]
    ms = jnp.mean(x * x, axis=-1, keepdims=True)
    y = x * lax.rsqrt(ms + EPS) * g_ref[...]
    o_ref[...] = (y * (1.0 + sc_ref[0]) + sh_ref[0]).astype(o_ref.dtype)


def _mod_router_kernel(x_ref, g_ref, sc_ref, sh_ref, wr_ref, br_ref, o_ref, ti_ref, wt_ref):
    x = x_ref[...]
    ms = jnp.mean(x * x, axis=-1, keepdims=True)
    y = x * lax.rsqrt(ms + EPS) * g_ref[...]
    h = y * (1.0 + sc_ref[0]) + sh_ref[0]
    h_hi = h.astype(BF16)
    o_ref[...] = h_hi
    h_lo = (h - h_hi.astype(F32)).astype(BF16)
    w = wr_ref[...]
    w_hi = w.astype(BF16)
    w_lo = (w - w_hi.astype(F32)).astype(BF16)
    logits = (jnp.dot(h_hi, w_hi, preferred_element_type=F32)
              + jnp.dot(h_hi, w_lo, preferred_element_type=F32)
              + jnp.dot(h_lo, w_hi, preferred_element_type=F32)) + br_ref[...]
    lane = lax.broadcasted_iota(jnp.int32, logits.shape, 1).astype(F32)
    vals, idxs = [], []
    cur = logits
    for _ in range(TOP_K):
        m = jnp.max(cur, axis=-1, keepdims=True)
        idx = jnp.min(jnp.where(cur == m, lane, float(LANES)), axis=-1, keepdims=True)
        vals.append(m)
        idxs.append(idx)
        cur = jnp.where(lane == idx, -jnp.inf, cur)
    es = [jnp.exp(v - vals[0]) for v in vals]
    den = es[0] + es[1] + es[2] + es[3]
    ti = jnp.zeros(logits.shape, F32)
    wt = jnp.zeros(logits.shape, F32)
    for k in range(TOP_K):
        ti = jnp.where(lane == float(k), idxs[k], ti)
        wt = jnp.where(lane == float(k), es[k] / den, wt)
    ti_ref[...] = ti.astype(jnp.int32)
    wt_ref[...] = wt


def modulate(x, g, sc3, sh3, seg, tm, w_r=None, b_r=None):
    t, d = x.shape
    in_specs = [
        pl.BlockSpec((tm, d), lambda i: (i, 0)),
        pl.BlockSpec((1, d), lambda i: (0, 0)),
        pl.BlockSpec((1, 1, d), lambda i: (seg(i), 0, 0)),
        pl.BlockSpec((1, 1, d), lambda i: (seg(i), 0, 0)),
    ]
    if w_r is None:
        return pl.pallas_call(
            _mod_kernel, grid=(t // tm,), in_specs=in_specs,
            out_specs=pl.BlockSpec((tm, d), lambda i: (i, 0)),
            out_shape=jax.ShapeDtypeStruct((t, d), BF16),
            compiler_params=_cp(("arbitrary",)), name="modulate",
        )(x, g.reshape(1, d), sc3, sh3)
    wr_pad = jnp.zeros((d, LANES), F32).at[:, :N_EXPERTS].set(w_r)
    br_pad = jnp.full((1, LANES), NEG, F32).at[0, :N_EXPERTS].set(b_r)
    in_specs += [pl.BlockSpec((d, LANES), lambda i: (0, 0)), pl.BlockSpec((1, LANES), lambda i: (0, 0))]
    return pl.pallas_call(
        _mod_router_kernel, grid=(t // tm,), in_specs=in_specs,
        out_specs=[pl.BlockSpec((tm, d), lambda i: (i, 0)),
                   pl.BlockSpec((tm, LANES), lambda i: (i, 0)),
                   pl.BlockSpec((tm, LANES), lambda i: (i, 0))],
        out_shape=[jax.ShapeDtypeStruct((t, d), BF16),
                   jax.ShapeDtypeStruct((t, LANES), jnp.int32),
                   jax.ShapeDtypeStruct((t, LANES), F32)],
        compiler_params=_cp(("arbitrary",)), name="modulate_router",
    )(x, g.reshape(1, d), sc3, sh3, wr_pad, br_pad)


def _mm_kernel(a_ref, b_ref, o_ref, acc_ref, *, nk):
    k = pl.program_id(2)

    @pl.when(k == 0)
    def _():
        acc_ref[...] = jnp.zeros_like(acc_ref)

    acc_ref[...] += jnp.dot(a_ref[...], b_ref[...], preferred_element_type=F32)

    @pl.when(k == nk - 1)
    def _():
        o_ref[...] = acc_ref[...].astype(o_ref.dtype)


def _mm_res_kernel(a_ref, b_ref, x_ref, g_ref, o_ref, acc_ref, *, nk):
    k = pl.program_id(2)

    @pl.when(k == 0)
    def _():
        acc_ref[...] = jnp.zeros_like(acc_ref)

    acc_ref[...] += jnp.dot(a_ref[...], b_ref[...], preferred_element_type=F32)

    @pl.when(k == nk - 1)
    def _():
        o_ref[...] = x_ref[...] + g_ref[0] * acc_ref[...]


def _pick(n, prefs):
    for p in prefs:
        if n % p == 0:
            return p
    raise ValueError(f"no tile for {n}")


def matmul(a, b, out_dtype=BF16, tm=512, tn=1024, tk=2048):
    m, kk = a.shape
    n = b.shape[1]
    tm, tn, tk = _pick(m, (tm, 256, 128)), _pick(n, (tn, 512, 256)), _pick(kk, (tk, 1024, 512, 256, 128))
    nk = kk // tk
    return pl.pallas_call(
        functools.partial(_mm_kernel, nk=nk),
        grid=(n // tn, m // tm, nk),
        in_specs=[pl.BlockSpec((tm, tk), lambda j, i, k: (i, k)),
                  pl.BlockSpec((tk, tn), lambda j, i, k: (k, j))],
        out_specs=pl.BlockSpec((tm, tn), lambda j, i, k: (i, j)),
        out_shape=jax.ShapeDtypeStruct((m, n), out_dtype),
        scratch_shapes=[pltpu.VMEM((tm, tn), F32)],
        compiler_params=_cp(("arbitrary", "arbitrary", "arbitrary")), name="matmul",
    )(a, b)


def matmul_residual(a, b, x, g3, seg, tm, tn=1024, tk=2048):
    m, kk = a.shape
    n = b.shape[1]
    tn, tk = _pick(n, (tn, 512, 256)), _pick(kk, (tk, 1024, 512))
    nk = kk // tk
    return pl.pallas_call(
        functools.partial(_mm_res_kernel, nk=nk),
        grid=(n // tn, m // tm, nk),
        in_specs=[pl.BlockSpec((tm, tk), lambda j, i, k: (i, k)),
                  pl.BlockSpec((tk, tn), lambda j, i, k: (k, j)),
                  pl.BlockSpec((tm, tn), lambda j, i, k: (i, j)),
                  pl.BlockSpec((1, 1, tn), lambda j, i, k: (seg(i), 0, j))],
        out_specs=pl.BlockSpec((tm, tn), lambda j, i, k: (i, j)),
        out_shape=jax.ShapeDtypeStruct((m, n), F32),
        scratch_shapes=[pltpu.VMEM((tm, tn), F32)],
        compiler_params=_cp(("arbitrary", "arbitrary", "arbitrary")), name="matmul_residual",
    )(a, b, x, g3)


def _bmm_kernel(a_ref, b_ref, o_ref, acc_ref, *, nk):
    k = pl.program_id(3)

    @pl.when(k == 0)
    def _():
        acc_ref[...] = jnp.zeros_like(acc_ref)

    acc_ref[...] += jnp.dot(a_ref[...], b_ref[0], preferred_element_type=F32)

    @pl.when(k == nk - 1)
    def _():
        o_ref[0] = acc_ref[...].astype(o_ref.dtype)


def shared_lhs_bmm(a, b, out_dtype=BF16):
    m, kk = a.shape
    nb, _, n = b.shape
    tm, tn, tk = _pick(m, (512, 256, 128)), _pick(n, (1024, 512)), _pick(kk, (2048, 1024, 512, 256))
    nk = kk // tk
    return pl.pallas_call(
        functools.partial(_bmm_kernel, nk=nk),
        grid=(nb, n // tn, m // tm, nk),
        in_specs=[pl.BlockSpec((tm, tk), lambda z, j, i, k: (i, k)),
                  pl.BlockSpec((1, tk, tn), lambda z, j, i, k: (z, k, j))],
        out_specs=pl.BlockSpec((1, tm, tn), lambda z, j, i, k: (z, i, j)),
        out_shape=jax.ShapeDtypeStruct((nb, m, n), out_dtype),
        scratch_shapes=[pltpu.VMEM((tm, tn), F32)],
        compiler_params=_cp(("arbitrary",) * 4), name="dft_seq_matmul",
    )(a, b)


def _even_a_kernel(u_ref, v_ref, gv_ref, ws_ref, bs_ref, o_ref, *, tm, heads):
    for h in range(heads):
        cs = slice(h * HEAD_DIM, (h + 1) * HEAD_DIM)
        u = _gelu_tanh(u_ref[:, cs].astype(F32))
        v = _gelu_tanh(v_ref[:, cs].astype(F32))
        v = v * lax.rsqrt(jnp.mean(v * v, axis=-1, keepdims=True) + EPS) * gv_ref[:, cs]
        vb = v.astype(BF16)
        w = ws_ref[h]
        bias = bs_ref[h]
        for c in range(tm // CHUNK):
            rs = slice(c * CHUNK, (c + 1) * CHUNK)
            mixed = jnp.dot(w, vb[rs], preferred_element_type=F32) + bias
            o_ref[rs, cs] = (u[rs] * mixed).astype(o_ref.dtype)


def even_a(p, g_v, w_s, b_s, tm):
    t = p.shape[0]
    heads = w_s.shape[0]
    aw = heads * HEAD_DIM
    bsb = jnp.broadcast_to(b_s[:, :, None], (heads, CHUNK, HEAD_DIM)).astype(F32)
    return pl.pallas_call(
        functools.partial(_even_a_kernel, tm=tm, heads=heads),
        grid=(t // tm,),
        in_specs=[pl.BlockSpec((tm, aw), lambda i: (i, 0)),
                  pl.BlockSpec((tm, aw), lambda i: (i, 1)),
                  pl.BlockSpec((1, aw), lambda i: (0, 0)),
                  pl.BlockSpec((heads, CHUNK, CHUNK), lambda i: (0, 0, 0)),
                  pl.BlockSpec((heads, CHUNK, HEAD_DIM), lambda i: (0, 0, 0))],
        out_specs=pl.BlockSpec((tm, aw), lambda i: (i, 0)),
        out_shape=jax.ShapeDtypeStruct((t, aw), BF16),
        compiler_params=_cp(("arbitrary",)), name="even_chunk_gmlp",
    )(p, p, g_v.reshape(1, aw), w_s.astype(BF16), bsb)


def _even_b1_kernel(z_ref, cs_ref, o_ref, *, groups):
    for g in range(groups):
        cs = slice(g * HEAD_DIM, (g + 1) * HEAD_DIM)
        r = jnp.dot(z_ref[:, cs], cs_ref[...], preferred_element_type=F32)
        o_ref[0, 0, :, cs] = r[:, :HEAD_DIM].astype(o_ref.dtype)
        o_ref[0, 1, :, cs] = r[:, HEAD_DIM:].astype(o_ref.dtype)


def even_b_feature_dft(p, cs_mat, row0, n, nb, tm):
    bw = p.shape[1] // 3
    groups = bw // HEAD_DIM
    tpb = n // tm
    t0 = row0 // tm
    return pl.pallas_call(
        functools.partial(_even_b1_kernel, groups=groups),
        grid=(nb, tpb),
        in_specs=[pl.BlockSpec((tm, bw), lambda b, j: (t0 + b * tpb + j, 2)),
                  pl.BlockSpec((HEAD_DIM, 2 * HEAD_DIM), lambda b, j: (0, 0))],
        out_specs=pl.BlockSpec((1, 2, tm, bw), lambda b, j: (b, 0, j, 0)),
        out_shape=jax.ShapeDtypeStruct((nb, 2, n, bw), BF16),
        compiler_params=_cp(("arbitrary", "arbitrary")), name="dft_feature",
    )(p, cs_mat)


def _dft_mats(n):
    k = jnp.arange(n, dtype=jnp.int32)
    kn = (k[:, None] * k[None, :]) % n
    ang = kn.astype(F32) * (2.0 * math.pi / n)
    s = 1.0 / math.sqrt(n)
    return jnp.cos(ang) * s, jnp.sin(ang) * s


def even_b(p, n_lat, n_ctx, nb, tm):
    c_d, s_d = _dft_mats(HEAD_DIM)
    cs_mat = jnp.concatenate([c_d, s_d], axis=1).astype(BF16)
    outs = []
    for row0, n in ((0, n_lat), (nb * n_lat, n_ctx)):
        if n == 0:
            continue
        zz = even_b_feature_dft(p, cs_mat, row0, n, nb, min(tm, n))
        c_n, s_n = _dft_mats(n)
        fcat = jnp.concatenate([c_n, -s_n], axis=1).astype(BF16)
        y = shared_lhs_bmm(fcat, zz.reshape(nb, 2 * n, zz.shape[-1]))
        outs.append(y.reshape(nb * n, -1))
    return outs


def _rope_kernel(q_ref, k_ref, cos_ref, sin_ref, o_ref, *, nblk):
    c = cos_ref[...]
    s = sin_ref[...]
    for src, base in ((q_ref, 0), (k_ref, nblk)):
        for j in range(nblk):
            t = src[:, j * HEAD_DIM:(j + 1) * HEAD_DIM].astype(F32)
            r = t * c + pltpu.roll(t, HEAD_DIM // 2, 1) * s
            o_ref[:, (base + j) * HEAD_DIM:(base + j + 1) * HEAD_DIM] = r.astype(o_ref.dtype)


def rope_qk(p, seq, nb, tm):
    dw = p.shape[1] // 6
    pos = jnp.arange(seq)
    row = (pos // GRID_W).astype(F32)
    col = (pos % GRID_W).astype(F32)
    nf = HEAD_DIM // 4
    inv = ROPE_THETA ** (-jnp.arange(nf, dtype=F32) / nf)
    ang = jnp.concatenate([row[:, None] * inv, col[:, None] * inv], axis=-1)
    cos, sin = jnp.cos(ang), jnp.sin(ang)
    cos2 = jnp.concatenate([cos, cos], axis=-1)
    sin2 = jnp.concatenate([-sin, sin], axis=-1)
    tpb = seq // tm
    return pl.pallas_call(
        functools.partial(_rope_kernel, nblk=dw // HEAD_DIM),
        grid=(nb * tpb,),
        in_specs=[pl.BlockSpec((tm, dw), lambda i: (i, 3)),
                  pl.BlockSpec((tm, dw), lambda i: (i, 4)),
                  pl.BlockSpec((tm, HEAD_DIM), lambda i: (i % tpb, 0)),
                  pl.BlockSpec((tm, HEAD_DIM), lambda i: (i % tpb, 0))],
        out_specs=pl.BlockSpec((tm, 2 * dw), lambda i: (i, 0)),
        out_shape=jax.ShapeDtypeStruct((nb * seq, 2 * dw), BF16),
        compiler_params=_cp(("arbitrary",)), name="rope",
    )(p, p, cos2, sin2)


def _diff_attn_kernel(*refs, seg_lens, kc, lambda_init):
    nseg = len(seg_lens)
    q_ref = refs[0]
    lam_ref, g_ref, o_ref = refs[1 + 2 * nseg:]
    lp = lam_ref[...]
    lam = (jnp.exp(jnp.sum(lp[0:1] * lp[1:2], axis=-1, keepdims=True))
           - jnp.exp(jnp.sum(lp[2:3] * lp[3:4], axis=-1, keepdims=True)) + lambda_init)
    tq = q_ref.shape[0]
    dv = o_ref.shape[1]
    qs = (q_ref[:, :HEAD_DIM], q_ref[:, HEAD_DIM:])
    m = [jnp.full((tq, 1), -jnp.inf, F32) for _ in range(2)]
    l = [jnp.zeros((tq, 1), F32) for _ in range(2)]
    acc = [jnp.zeros((tq, dv), F32) for _ in range(2)]
    for si in range(nseg):
        k_ref, v_ref = refs[1 + 2 * si], refs[2 + 2 * si]
        n = seg_lens[si]
        step = min(kc, n)
        for c0 in range(0, n, step):
            kblk = k_ref[c0:c0 + step, :]
            vblk = v_ref[c0:c0 + step, :]
            for t in range(2):
                s = lax.dot_general(qs[t], kblk[:, t * HEAD_DIM:(t + 1) * HEAD_DIM],
                                    (((1,), (1,)), ((), ())), preferred_element_type=F32) * ATTN_SCALE
                m_new = jnp.maximum(m[t], jnp.max(s, axis=-1, keepdims=True))
                alpha = jnp.exp(m[t] - m_new)
                pr = jnp.exp(s - m_new)
                l[t] = alpha * l[t] + jnp.sum(pr, axis=-1, keepdims=True)
                acc[t] = alpha * acc[t] + jnp.dot(pr.astype(BF16), vblk, preferred_element_type=F32)
                m[t] = m_new
    o = acc[0] / l[0] - lam * (acc[1] / l[1])
    o = o * lax.rsqrt(jnp.mean(o * o, axis=-1, keepdims=True) + EPS) * g_ref[...]
    o_ref[...] = (o * (1.0 - lambda_init)).astype(o_ref.dtype)


def diff_attention(q_arr, q_row0, q_col0, nq, segs, lam_p, g_sub, nb, heads, lambda_init, tq):
    dv = 2 * HEAD_DIM
    tq = min(tq, nq)
    qpb = nq // tq
    in_specs = [pl.BlockSpec((tq, dv), lambda b, h, i: (q_row0 // tq + b * qpb + i, q_col0 + h))]
    args = [q_arr]
    seg_lens = []
    for (k_arr, k_row0, k_col0, v_arr, v_row0, v_col0, n) in segs:
        in_specs.append(pl.BlockSpec((n, dv), functools.partial(
            lambda b, h, i, r0, c0, nn: (r0 // nn + b, c0 + h), r0=k_row0, c0=k_col0, nn=n)))
        in_specs.append(pl.BlockSpec((n, dv), functools.partial(
            lambda b, h, i, r0, c0, nn: (r0 // nn + b, c0 + h), r0=v_row0, c0=v_col0, nn=n)))
        args += [k_arr, v_arr]
        seg_lens.append(n)
    in_specs += [pl.BlockSpec((4, HEAD_DIM), lambda b, h, i: (0, 0)),
                 pl.BlockSpec((1, dv), lambda b, h, i: (0, 0))]
    args += [lam_p.astype(F32), g_sub.reshape(1, dv)]
    return pl.pallas_call(
        functools.partial(_diff_attn_kernel, seg_lens=tuple(seg_lens), kc=1024, lambda_init=lambda_init),
        grid=(nb, heads, qpb),
        in_specs=in_specs,
        out_specs=pl.BlockSpec((tq, dv), lambda b, h, i: (b * qpb + i, h)),
        out_shape=jax.ShapeDtypeStruct((nb * nq, heads * dv), BF16),
        compiler_params=_cp(("arbitrary",) * 3), name="diff_attention",
    )(*args)


def _na_geometry(rows):
    kh = min(NA_KH, rows)
    qr = math.gcd(rows, NA_QR)
    kb = min(kh + qr - 1, rows)
    n_rb = rows // qr
    band_r = np.clip(np.arange(n_rb) * qr - kh // 2, 0, rows - kb)
    return kh, qr, kb, n_rb, band_r


def _na_bias(rpb, rows):
    kh, qr, kb, n_rb, band_r = _na_geometry(rows)
    rel = band_r - np.arange(n_rb) * qr
    cases, case_of = [], []
    for i in range(n_rb):
        q_row = i * qr + np.arange(qr)
        win_r = np.clip(q_row - kh // 2, 0, rows - kh)
        key = (int(rel[i]), tuple((win_r - i * qr).tolist()))
        if key not in cases:
            cases.append(key)
        case_of.append(cases.index(key))
    q_col = np.arange(GRID_W)
    key_c = np.arange(GRID_W)
    win_c = np.clip(q_col - NA_KW // 2, 0, GRID_W - NA_KW)
    ok_c = (key_c[None, :] >= win_c[:, None]) & (key_c[None, :] < win_c[:, None] + NA_KW)
    dc = np.clip(key_c[None, :] - q_col[:, None], 1 - NA_KW, NA_KW - 1) + NA_KW - 1
    tabs = []
    for (r, win_rel) in cases:
        q_row = np.arange(qr)
        key_r = r + np.arange(kb)
        win_r = np.asarray(win_rel)
        ok_r = (key_r[None, :] >= win_r[:, None]) & (key_r[None, :] < win_r[:, None] + kh)
        dr = np.clip(key_r[None, :] - q_row[:, None], 1 - NA_KH, NA_KH - 1) + NA_KH - 1
        dr4 = np.broadcast_to(dr[:, None, :, None], (qr, GRID_W, kb, GRID_W))
        dc4 = np.broadcast_to(dc[None, :, None, :], (qr, GRID_W, kb, GRID_W))
        ok4 = ok_r[:, None, :, None] & ok_c[None, :, None, :]
        b = rpb[:, dr4.reshape(qr * GRID_W, kb * GRID_W), dc4.reshape(qr * GRID_W, kb * GRID_W)]
        tabs.append(jnp.where(ok4.reshape(qr * GRID_W, kb * GRID_W)[None], b.astype(F32), NEG))
    return jnp.stack(tabs, axis=1), np.asarray(case_of, np.int32), band_r.astype(np.int32)


def _na_kernel(case_ref, start_ref, q_ref, k_ref, v_ref, kx_ref, vx_ref, bias_ref, o_ref, *, nkeys):
    i = pl.program_id(2)
    start = pl.multiple_of(start_ref[i], GRID_W)
    q = q_ref[...]
    kb = k_ref[pl.ds(start, nkeys), :]
    vb = v_ref[pl.ds(start, nkeys), :]
    dn = (((1,), (1,)), ((), ()))
    s_loc = lax.dot_general(q, kb, dn, preferred_element_type=F32) * ATTN_SCALE + bias_ref[0, 0]
    s_ctx = lax.dot_general(q, kx_ref[...], dn, preferred_element_type=F32) * ATTN_SCALE
    m = jnp.maximum(jnp.max(s_loc, axis=-1, keepdims=True), jnp.max(s_ctx, axis=-1, keepdims=True))
    p_loc = jnp.exp(s_loc - m)
    p_ctx = jnp.exp(s_ctx - m)
    den = jnp.sum(p_loc, axis=-1, keepdims=True) + jnp.sum(p_ctx, axis=-1, keepdims=True)
    o = (jnp.dot(p_loc.astype(BF16), vb, preferred_element_type=F32)
         + jnp.dot(p_ctx.astype(BF16), vx_ref[...], preferred_element_type=F32))
    o_ref[...] = (o / den).astype(o_ref.dtype)


def neighbourhood_attention(p, rpb, seq, n_ctx, nb, heads):
    rows = seq // GRID_W
    kh, qr, kb, n_rb, _ = _na_geometry(rows)
    bias, case_of, band_r = _na_bias(rpb, rows)
    tq = qr * GRID_W
    nkeys = kb * GRID_W
    qpb = seq // tq
    xb0 = nb * seq // n_ctx
    grid_spec = pltpu.PrefetchScalarGridSpec(
        num_scalar_prefetch=2,
        grid=(nb, heads, n_rb),
        in_specs=[
            pl.BlockSpec((tq, HEAD_DIM), lambda b, h, i, cs, st: (b * qpb + i, h)),
            pl.BlockSpec((seq, HEAD_DIM), lambda b, h, i, cs, st: (b, heads + h)),
            pl.BlockSpec((seq, HEAD_DIM), lambda b, h, i, cs, st: (b, 2 * heads + h)),
            pl.BlockSpec((n_ctx, HEAD_DIM), lambda b, h, i, cs, st: (xb0 + b, heads + h)),
            pl.BlockSpec((n_ctx, HEAD_DIM), lambda b, h, i, cs, st: (xb0 + b, 2 * heads + h)),
            pl.BlockSpec((1, 1, tq, nkeys), lambda b, h, i, cs, st: (h, cs[i], 0, 0)),
        ],
        out_specs=pl.BlockSpec((tq, HEAD_DIM), lambda b, h, i, cs, st: (b * qpb + i, h)),
    )
    return pl.pallas_call(
        functools.partial(_na_kernel, nkeys=nkeys),
        grid_spec=grid_spec,
        out_shape=jax.ShapeDtypeStruct((nb * seq, heads * HEAD_DIM), BF16),
        compiler_params=_cp(("arbitrary",) * 3), name="neighbourhood_attention",
    )(jnp.asarray(case_of), jnp.asarray(band_r * GRID_W), p, p, p, p, p, bias)


def _ctx_attn_kernel(q_ref, k_ref, v_ref, o_ref):
    s = lax.dot_general(q_ref[...], k_ref[...], (((1,), (1,)), ((), ())),
                        preferred_element_type=F32) * ATTN_SCALE
    m = jnp.max(s, axis=-1, keepdims=True)
    pr = jnp.exp(s - m)
    den = jnp.sum(pr, axis=-1, keepdims=True)
    o = jnp.dot(pr.astype(BF16), v_ref[...], preferred_element_type=F32)
    o_ref[...] = (o / den).astype(o_ref.dtype)


def ctx_dense_attention(p, row0, n_ctx, nb, heads):
    xb0 = row0 // n_ctx
    return pl.pallas_call(
        _ctx_attn_kernel,
        grid=(nb, heads),
        in_specs=[pl.BlockSpec((n_ctx, HEAD_DIM), lambda b, h: (xb0 + b, h)),
                  pl.BlockSpec((n_ctx, HEAD_DIM), lambda b, h: (xb0 + b, heads + h)),
                  pl.BlockSpec((n_ctx, HEAD_DIM), lambda b, h: (xb0 + b, 2 * heads + h))],
        out_specs=pl.BlockSpec((n_ctx, HEAD_DIM), lambda b, h: (b, h)),
        out_shape=jax.ShapeDtypeStruct((nb * n_ctx, heads * HEAD_DIM), BF16),
        compiler_params=_cp(("arbitrary", "arbitrary")), name="ctx_dense_attention",
    )(p, p, p)


MOE_TM = 256


def _expert_kernel(te_ref, tv_ref, x_ref, wgu_ref, bgu_ref, wdn_ref, bdn_ref, o_ref, *, de):
    i = pl.program_id(0)

    @pl.when(tv_ref[i] != 0)
    def _():
        gu = jnp.dot(x_ref[...], wgu_ref[0], preferred_element_type=F32) + bgu_ref[0]
        gate = jnp.minimum(gu[:, :de], SWIGLU_LIMIT)
        up = jnp.clip(gu[:, de:], -SWIGLU_LIMIT, SWIGLU_LIMIT)
        act = gate * _sigmoid(SWIGLU_ALPHA * gate) * (up + 1.0)
        y = jnp.dot(act.astype(BF16), wdn_ref[0], preferred_element_type=F32) + bdn_ref[0]
        o_ref[...] = y.astype(o_ref.dtype)

    @pl.when(tv_ref[i] == 0)
    def _():
        o_ref[...] = jnp.zeros_like(o_ref)


def expert_ffn(xs, tile_e, tile_valid, w_gu, b_gu, w_dn, b_dn):
    nr, d = xs.shape
    ne, _, f2 = w_gu.shape
    de = f2 // 2
    tm = MOE_TM
    grid_spec = pltpu.PrefetchScalarGridSpec(
        num_scalar_prefetch=2,
        grid=(nr // tm,),
        in_specs=[
            pl.BlockSpec((tm, d), lambda i, te, tv: (i, 0)),
            pl.BlockSpec((1, d, f2), lambda i, te, tv: (te[i], 0, 0)),
            pl.BlockSpec((1, 1, f2), lambda i, te, tv: (te[i], 0, 0)),
            pl.BlockSpec((1, de, d), lambda i, te, tv: (te[i], 0, 0)),
            pl.BlockSpec((1, 1, d), lambda i, te, tv: (te[i], 0, 0)),
        ],
        out_specs=pl.BlockSpec((tm, d), lambda i, te, tv: (i, 0)),
    )
    return pl.pallas_call(
        functools.partial(_expert_kernel, de=de),
        grid_spec=grid_spec,
        out_shape=jax.ShapeDtypeStruct((nr, d), BF16),
        compiler_params=_cp(("arbitrary",)), name="expert_ffn",
    )(tile_e, tile_valid, xs, w_gu, b_gu.reshape(ne, 1, f2), w_dn, b_dn.reshape(ne, 1, d))


def _combine_kernel(x_ref, y_ref, wt_ref, g_ref, o_ref):
    wt = wt_ref[...]
    acc = wt[:, 0:1] * y_ref[0].astype(F32)
    for k in range(1, TOP_K):
        acc = acc + wt[:, k:k + 1] * y_ref[k].astype(F32)
    o_ref[...] = x_ref[...] + g_ref[0] * acc


def _combine_norm_kernel(x_ref, y_ref, wt_ref, g_ref, nf_ref, o_ref):
    wt = wt_ref[...]
    acc = wt[:, 0:1] * y_ref[0].astype(F32)
    for k in range(1, TOP_K):
        acc = acc + wt[:, k:k + 1] * y_ref[k].astype(F32)
    x = x_ref[...] + g_ref[0] * acc
    o_ref[...] = x * lax.rsqrt(jnp.mean(x * x, axis=-1, keepdims=True) + EPS) * nf_ref[...]


def moe_combine(x, yg, wts, g3, seg, tm, norm_final=None):
    t, d = x.shape
    in_specs = [pl.BlockSpec((tm, d), lambda i: (i, 0)),
                pl.BlockSpec((TOP_K, tm, d), lambda i: (0, i, 0)),
                pl.BlockSpec((tm, LANES), lambda i: (i, 0)),
                pl.BlockSpec((1, 1, d), lambda i: (seg(i), 0, 0))]
    args = [x, yg, wts, g3]
    kern = _combine_kernel
    if norm_final is not None:
        in_specs.append(pl.BlockSpec((1, d), lambda i: (0, 0)))
        args.append(norm_final.reshape(1, d))
        kern = _combine_norm_kernel
    return pl.pallas_call(
        kern, grid=(t // tm,), in_specs=in_specs,
        out_specs=pl.BlockSpec((tm, d), lambda i: (i, 0)),
        out_shape=jax.ShapeDtypeStruct((t, d), F32),
        compiler_params=_cp(("arbitrary",)), name="moe_combine",
    )(*args)


def moe_layer(x, g_ffn, sc3, sh3, g3, seg, tm, w_r, b_r, w_gu, b_gu, w_dn, b_dn, norm_final=None):
    t, d = x.shape
    h, topi, wts = modulate(x, g_ffn, sc3, sh3, seg, tm, w_r, b_r)
    tme = MOE_TM
    ns = t * TOP_K
    nt = -(-(ns + N_EXPERTS * (tme - 1)) // tme)
    ef = topi[:, :TOP_K].reshape(ns)
    onehot = (ef[:, None] == jnp.arange(N_EXPERTS, dtype=jnp.int32)[None, :]).astype(jnp.int32)
    csum = jnp.cumsum(onehot, axis=0)
    rank = jnp.sum(csum * onehot, axis=1) - 1
    counts = csum[-1]
    padded = ((counts + tme - 1) // tme) * tme
    ends = jnp.cumsum(padded)
    offs = ends - padded
    pos = offs[ef] + rank
    row_tok = jnp.zeros((nt * tme,), jnp.int32).at[pos].set(jnp.arange(ns, dtype=jnp.int32) // TOP_K)
    tile_start = jnp.arange(nt, dtype=jnp.int32) * tme
    tile_valid = (tile_start < ends[-1]).astype(jnp.int32)
    tile_e = jnp.minimum(jnp.searchsorted(ends, tile_start, side="right").astype(jnp.int32), N_EXPERTS - 1)
    tile_e = jnp.where(tile_valid != 0, tile_e, jnp.max(jnp.where(tile_valid != 0, tile_e, 0)))
    xs = jnp.take(h, row_tok, axis=0)
    ys = expert_ffn(xs, tile_e, tile_valid, w_gu.astype(BF16), b_gu, w_dn.astype(BF16), b_dn)
    yg = jnp.take(ys, pos.reshape(t, TOP_K).T, axis=0)
    return moe_combine(x, yg, wts, g3, seg, tm, norm_final)


def kernel(x, c, ctx, c_ctx, ada_w1, ada_w2, ada_b, norm_mix, norm_ffn, ev_w_in, ev_v_norm, ev_w_s, ev_b_s, ev_w_out, od_w_in, od_rpb, od_lam, od_sub_norm, od_w_out, moe_w_r, moe_b_r, moe_w_gu, moe_b_gu, moe_w_dn, moe_b_dn, norm_final):
    nb, seq, d = x.shape
    n_ctx = ctx.shape[1]
    depth = ada_w1.shape[0]
    n_lat = nb * seq
    t_all = n_lat + nb * n_ctx
    tm = 512 if (t_all % 512 == 0 and seq % 512 == 0) else 256
    tms = 256
    assert n_lat % tm == 0 and (nb * n_ctx) % tms == 0 and seq % tms == 0 and nb + 1 <= 8

    def seg_for(tile):
        return _seg_fn(n_lat // tile, seq // tile, nb)

    xs = jnp.concatenate([x.reshape(n_lat, d), ctx.reshape(nb * n_ctx, d)], axis=0)

    cv = jnp.zeros((8, d), F32).at[:nb].set(c).at[nb].set(c_ctx)
    mods = ada_all(cv, ada_w1, ada_w2, ada_b)

    for i in range(depth):
        last = i == depth - 1
        j = i // 2
        m6 = mods[i].reshape(8, 6, d)
        sh1, sc1, g1, sh2, sc2, g2 = [m6[:, q, :].reshape(8, 1, d) for q in range(6)]
        h = modulate(xs, norm_mix[i], sc1, sh1, seg_for(tms), tms)
        if i % 2 == 0:
            p = matmul(h, ev_w_in[j].astype(BF16))
            ya = even_a(p, ev_v_norm[j], ev_w_s[j], ev_b_s[j], tms)
            yb = even_b(p, seq, n_ctx, nb, tms)
            y = jnp.concatenate([ya, jnp.concatenate(yb, axis=0)], axis=1)
            w_out = ev_w_out[j]
            n_out = t_all
        else:
            lambda_init = 0.8 - 0.6 * math.exp(-0.3 * i)
            p = matmul(h, od_w_in[j].astype(BF16))
            cw = p.shape[1] // 6
            heads_c = cw // HEAD_DIM
            heads_d = cw // (2 * HEAD_DIM)
            o_c = neighbourhood_attention(p, od_rpb[j], seq, n_ctx, nb, heads_c)
            qk = rope_qk(p, seq, nb, tms)
            nblk = cw // (2 * HEAD_DIM)
            segs = [(p, n_lat, 4 * nblk, p, n_lat, 5 * nblk, n_ctx),
                    (qk, 0, nblk, p, 0, 5 * nblk, seq)]
            o_d = diff_attention(qk, 0, 0, seq, segs, od_lam[j], od_sub_norm[j], nb, heads_d, lambda_init, 256)
            y = jnp.concatenate([o_c, o_d], axis=1)
            n_out = n_lat
            if not last:
                o_cx = ctx_dense_attention(p, n_lat, n_ctx, nb, heads_c)
                segs_x = [(p, n_lat, 4 * nblk, p, n_lat, 5 * nblk, n_ctx)]
                o_dx = diff_attention(p, n_lat, 3 * nblk, n_ctx, segs_x, od_lam[j], od_sub_norm[j], nb,
                                      heads_d, lambda_init, 256)
                y = jnp.concatenate([y, jnp.concatenate([o_cx, o_dx], axis=1)], axis=0)
                n_out = t_all
            w_out = od_w_out[j]
        if n_out != t_all:
            xs = xs[:n_out]
        xs = matmul_residual(y, w_out.astype(BF16), xs, g1, seg_for(tm), tm)
        xs = moe_layer(xs, norm_ffn[i], sc2, sh2, g2, seg_for(tms), tms, moe_w_r[i], moe_b_r[i],
                       moe_w_gu[i], moe_b_gu[i], moe_w_dn[i], moe_b_dn[i],
                       norm_final if last else None)
    return xs[:n_lat].reshape(nb, seq, d)
```

```python
import functools
import math

import numpy as np
import jax
import jax.numpy as jnp
from jax import lax
from jax.experimental import pallas as pl
from jax.experimental.pallas import tpu as pltpu

F32 = jnp.float32
BF16 = jnp.bfloat16

GRID_W = 64
HEAD_DIM = 128
EPS = 1e-6
ROPE_THETA = 10000.0
CHUNK = 128
NA_KH = 8
NA_KW = 16
NA_QR = 8
ATTN_SCALE = HEAD_DIM ** -0.5
NEG = -1e30
N_EXPERTS = 32
TOP_K = 4
SWIGLU_LIMIT = 7.0
SWIGLU_ALPHA = 1.702
LANES = 128

VMEM_LIMIT = 56 * 1024 * 1024


def _cp(sem, vmem=VMEM_LIMIT):
    return pltpu.CompilerParams(dimension_semantics=sem, vmem_limit_bytes=vmem)


def _gelu_tanh(x):
    return 0.5 * x * (1.0 + jnp.tanh(0.7978845608028654 * (x + 0.044715 * (x * x * x))))


def _sigmoid(x):
    return 1.0 / (1.0 + jnp.exp(-x))


def _pack_halves(x):
    c = x.shape[1] // 2
    lo = lax.shift_right_logical(lax.bitcast_convert_type(x[:, :c], jnp.uint32), jnp.uint32(16))
    hi = lax.bitcast_convert_type(x[:, c:], jnp.uint32) & jnp.uint32(0xFFFF0000)
    return lo | hi


def _unpack_halves(w):
    lo = lax.bitcast_convert_type(lax.shift_left(w, jnp.uint32(16)), F32)
    hi = lax.bitcast_convert_type(w & jnp.uint32(0xFFFF0000), F32)
    return lo, hi


def _pick(n, prefs):
    for p in prefs:
        if n % p == 0:
            return p
    raise ValueError(f"no tile for {n}")


def _ada_kernel(cv_ref, w1_ref, w2_ref, b_ref, o_ref):
    cv = cv_ref[...]
    s = cv * _sigmoid(cv)
    t = jnp.dot(s, w1_ref[0], preferred_element_type=F32, precision=lax.Precision.HIGHEST)
    m = jnp.dot(t, w2_ref[0], preferred_element_type=F32, precision=lax.Precision.HIGHEST)
    o_ref[0] = m + b_ref[0]


def ada_all(cv, w1, w2, b):
    nl, d, r = w1.shape
    n6 = w2.shape[2]
    tn = 2048
    return pl.pallas_call(
        _ada_kernel,
        grid=(nl, n6 // tn),
        in_specs=[
            pl.BlockSpec((8, d), lambda l, j: (0, 0)),
            pl.BlockSpec((1, d, r), lambda l, j: (l, 0, 0)),
            pl.BlockSpec((1, r, tn), lambda l, j: (l, 0, j)),
            pl.BlockSpec((1, 1, tn), lambda l, j: (l, 0, j)),
        ],
        out_specs=pl.BlockSpec((1, 8, tn), lambda l, j: (l, 0, j)),
        out_shape=jax.ShapeDtypeStruct((nl, 8, n6), F32),
        compiler_params=_cp(("arbitrary", "arbitrary")),
        name="ada_mod",
    )(cv, w1, w2, b.reshape(nl, 1, n6))


def _seg_fn(n_lat_tiles, tiles_per_batch, n_batch):
    def seg(i):
        return jnp.where(i < n_lat_tiles, i // tiles_per_batch, n_batch)
    return seg


def _mod_kernel(x_ref, g_ref, sc_ref, sh_ref, o_ref):
    x = x_ref[...]
    ms = jnp.mean(x * x, axis=-1, keepdims=True)
    y = x * lax.rsqrt(ms + EPS) * g_ref[...]
    o_ref[...] = (y * (1.0 + sc_ref[0]) + sh_ref[0]).astype(o_ref.dtype)


def _mod_router_kernel(x_ref, g_ref, sc_ref, sh_ref, wr_ref, br_ref, o_ref, ti_ref, wt_ref, rk_ref, cnt_ref,
                       base_ref):
    @pl.when(pl.program_id(0) == 0)
    def _():
        base_ref[...] = jnp.zeros_like(base_ref)

    x = x_ref[...]
    ms = jnp.mean(x * x, axis=-1, keepdims=True)
    y = x * lax.rsqrt(ms + EPS) * g_ref[...]
    h = y * (1.0 + sc_ref[0]) + sh_ref[0]
    h_hi = h.astype(BF16)
    o_ref[...] = _pack_halves(h_hi.astype(F32))
    h_lo = (h - h_hi.astype(F32)).astype(BF16)
    w = wr_ref[...]
    w_hi = w.astype(BF16)
    w_lo = (w - w_hi.astype(F32)).astype(BF16)
    logits = (jnp.dot(h_hi, w_hi, preferred_element_type=F32)
              + jnp.dot(h_hi, w_lo, preferred_element_type=F32)
              + jnp.dot(h_lo, w_hi, preferred_element_type=F32)) + br_ref[...]
    tm = logits.shape[0]
    lane = lax.broadcasted_iota(jnp.int32, logits.shape, 1).astype(F32)
    vals, idxs = [], []
    cur = logits
    for _ in range(TOP_K):
        m = jnp.max(cur, axis=-1, keepdims=True)
        idx = jnp.min(jnp.where(cur == m, lane, float(LANES)), axis=-1, keepdims=True)
        vals.append(m)
        idxs.append(idx)
        cur = jnp.where(lane == idx, -jnp.inf, cur)
    es = [jnp.exp(v - vals[0]) for v in vals]
    den = es[0] + es[1] + es[2] + es[3]
    sel = [lane == idxs[k] for k in range(TOP_K)]
    osum = jnp.zeros(logits.shape, F32)
    for k in range(TOP_K):
        osum = osum + jnp.where(sel[k], 1.0, 0.0)
    r_i = lax.broadcasted_iota(jnp.int32, (tm, tm), 0)
    c_i = lax.broadcasted_iota(jnp.int32, (tm, tm), 1)
    lower = jnp.where(c_i < r_i, 1.0, 0.0).astype(BF16)
    pref = jnp.dot(lower, osum.astype(BF16), preferred_element_type=F32) + base_ref[...]
    ti = jnp.zeros(logits.shape, F32)
    wt = jnp.zeros(logits.shape, F32)
    rk = jnp.zeros(logits.shape, F32)
    for k in range(TOP_K):
        rank_k = jnp.sum(jnp.where(sel[k], pref, 0.0), axis=-1, keepdims=True)
        ti = jnp.where(lane == float(k), idxs[k], ti)
        wt = jnp.where(lane == float(k), es[k] / den, wt)
        rk = jnp.where(lane == float(k), rank_k, rk)
    ti_ref[...] = ti.astype(jnp.int32)
    wt_ref[...] = wt
    rk_ref[...] = rk.astype(jnp.int32)
    base_ref[...] += jnp.sum(osum, axis=0, keepdims=True)
    cnt_ref[...] = base_ref[...].astype(jnp.int32)


def modulate(x, g, sc3, sh3, seg, tm, m_rows=None, w_r=None, b_r=None):
    t, d = x.shape
    t = t if m_rows is None else m_rows
    in_specs = [
        pl.BlockSpec((tm, d), lambda i: (i, 0)),
        pl.BlockSpec((1, d), lambda i: (0, 0)),
        pl.BlockSpec((1, 1, d), lambda i: (seg(i), 0, 0)),
        pl.BlockSpec((1, 1, d), lambda i: (seg(i), 0, 0)),
    ]
    if w_r is None:
        return pl.pallas_call(
            _mod_kernel, grid=(t // tm,), in_specs=in_specs,
            out_specs=pl.BlockSpec((tm, d), lambda i: (i, 0)),
            out_shape=jax.ShapeDtypeStruct((t, d), BF16),
            compiler_params=_cp(("arbitrary",)), name="modulate",
        )(x, g.reshape(1, d), sc3, sh3)
    wr_pad = jnp.zeros((d, LANES), F32).at[:, :N_EXPERTS].set(w_r)
    br_pad = jnp.full((1, LANES), NEG, F32).at[0, :N_EXPERTS].set(b_r)
    in_specs += [pl.BlockSpec((d, LANES), lambda i: (0, 0)), pl.BlockSpec((1, LANES), lambda i: (0, 0))]
    row_spec = pl.BlockSpec((tm, LANES), lambda i: (i, 0))
    return pl.pallas_call(
        _mod_router_kernel, grid=(t // tm,), in_specs=in_specs,
        out_specs=[pl.BlockSpec((tm, d // 2), lambda i: (i, 0)), row_spec, row_spec, row_spec,
                   pl.BlockSpec((1, LANES), lambda i: (0, 0))],
        out_shape=[jax.ShapeDtypeStruct((t, d // 2), jnp.uint32),
                   jax.ShapeDtypeStruct((t, LANES), jnp.int32),
                   jax.ShapeDtypeStruct((t, LANES), F32),
                   jax.ShapeDtypeStruct((t, LANES), jnp.int32),
                   jax.ShapeDtypeStruct((1, LANES), jnp.int32)],
        scratch_shapes=[pltpu.VMEM((1, LANES), F32)],
        compiler_params=_cp(("arbitrary",)), name="modulate_router",
    )(x, g.reshape(1, d), sc3, sh3, wr_pad, br_pad)


def _cast_kernel(x_ref, o_ref):
    o_ref[...] = x_ref[...].astype(o_ref.dtype)


def cast_bf16(w, block_rows):
    shape = w.shape
    w2 = w.reshape(-1, shape[-1])
    r, c = w2.shape
    out = pl.pallas_call(
        _cast_kernel, grid=(r // block_rows,),
        in_specs=[pl.BlockSpec((block_rows, c), lambda i: (i, 0))],
        out_specs=pl.BlockSpec((block_rows, c), lambda i: (i, 0)),
        out_shape=jax.ShapeDtypeStruct((r, c), BF16),
        compiler_params=_cp(("arbitrary",)), name="cast_bf16",
    )(w2)
    return out.reshape(shape)


def _mm_kernel(a_ref, b_ref, o_ref):
    o_ref[...] = jnp.dot(a_ref[...], b_ref[0], preferred_element_type=F32).astype(o_ref.dtype)


def matmul(a, b3, layer, tm, out_dtype=BF16, tn=1024):
    m, kk = a.shape
    n = b3.shape[2]
    tn = _pick(n, (tn, 512, 256))
    return pl.pallas_call(
        _mm_kernel,
        grid=(n // tn, m // tm),
        in_specs=[pl.BlockSpec((tm, kk), lambda j, i: (i, 0)),
                  pl.BlockSpec((1, kk, tn), lambda j, i: (layer, 0, j))],
        out_specs=pl.BlockSpec((tm, tn), lambda j, i: (i, j)),
        out_shape=jax.ShapeDtypeStruct((m, n), out_dtype),
        compiler_params=_cp(("arbitrary", "arbitrary")), name="matmul",
    )(a, b3)


def _mm_res_kernel(a1_ref, a2_ref, b_ref, x_ref, g_ref, o_ref):
    k1 = a1_ref.shape[1]
    acc = jnp.dot(a1_ref[...], b_ref[0, :k1, :], preferred_element_type=F32)
    acc = acc + jnp.dot(a2_ref[...], b_ref[0, k1:, :], preferred_element_type=F32)
    o_ref[...] = x_ref[...] + g_ref[0] * acc


def matmul_residual(a1, a2, b3, layer, x, g3, seg, tm, m_rows, tn=1024):
    k1, k2 = a1.shape[1], a2.shape[1]
    n = b3.shape[2]
    tn = _pick(n, (tn, 512, 256))
    return pl.pallas_call(
        _mm_res_kernel,
        grid=(n // tn, m_rows // tm),
        in_specs=[pl.BlockSpec((tm, k1), lambda j, i: (i, 0)),
                  pl.BlockSpec((tm, k2), lambda j, i: (i, 0)),
                  pl.BlockSpec((1, k1 + k2, tn), lambda j, i: (layer, 0, j)),
                  pl.BlockSpec((tm, tn), lambda j, i: (i, j)),
                  pl.BlockSpec((1, 1, tn), lambda j, i: (seg(i), 0, j))],
        out_specs=pl.BlockSpec((tm, tn), lambda j, i: (i, j)),
        out_shape=jax.ShapeDtypeStruct((m_rows, n), F32),
        compiler_params=_cp(("arbitrary", "arbitrary")), name="matmul_residual",
    )(a1, a2, b3, x, g3)


def _bmm_kernel(a_ref, b_ref, *rest, nk):
    o_ref, acc_ref = rest[-2], rest[-1]
    k = pl.program_id(3)

    @pl.when(k == 0)
    def _():
        acc_ref[...] = jnp.zeros_like(acc_ref)

    acc_ref[...] += jnp.dot(a_ref[...], b_ref[0], preferred_element_type=F32)

    @pl.when(k == nk - 1)
    def _():
        o_ref[...] = acc_ref[...].astype(o_ref.dtype)


def shared_lhs_bmm(a, b, out_rows, row0, prev=None):
    m, kk = a.shape
    nb, _, n = b.shape
    tm, tn, tk = _pick(m, (512, 256, 128)), _pick(n, (1024, 512)), _pick(kk, (4096, 2048, 1024, 512, 256))
    nk = kk // tk
    r0 = row0 // tm
    mt = m // tm
    in_specs = [pl.BlockSpec((tm, tk), lambda z, j, i, k: (i, k)),
                pl.BlockSpec((1, tk, tn), lambda z, j, i, k: (z, k, j))]
    args = [a, b]
    aliases = {}
    if prev is not None:
        in_specs.append(pl.BlockSpec(memory_space=pl.ANY))
        args.append(prev)
        aliases = {2: 0}
    return pl.pallas_call(
        functools.partial(_bmm_kernel, nk=nk),
        grid=(nb, n // tn, mt, nk),
        in_specs=in_specs,
        out_specs=pl.BlockSpec((tm, tn), lambda z, j, i, k: (r0 + z * mt + i, j)),
        out_shape=jax.ShapeDtypeStruct((out_rows, n), BF16),
        scratch_shapes=[pltpu.VMEM((tm, tn), F32)],
        input_output_aliases=aliases,
        compiler_params=_cp(("arbitrary",) * 4), name="dft_seq_matmul",
    )(*args)


def _even_a_kernel(u_ref, v_ref, gv_ref, ws_ref, bs_ref, o_ref, *, tm, heads):
    for h in range(heads):
        cs = slice(h * HEAD_DIM, (h + 1) * HEAD_DIM)
        u = _gelu_tanh(u_ref[:, cs].astype(F32))
        v = _gelu_tanh(v_ref[:, cs].astype(F32))
        v = v * lax.rsqrt(jnp.mean(v * v, axis=-1, keepdims=True) + EPS) * gv_ref[:, cs]
        vb = v.astype(BF16)
        w = ws_ref[h]
        bias = bs_ref[h]
        for c in range(tm // CHUNK):
            rs = slice(c * CHUNK, (c + 1) * CHUNK)
            mixed = jnp.dot(w, vb[rs], preferred_element_type=F32) + bias
            o_ref[rs, cs] = (u[rs] * mixed).astype(o_ref.dtype)


def even_a(p, g_v, w_s, b_s, tm):
    t = p.shape[0]
    heads = w_s.shape[0]
    aw = heads * HEAD_DIM
    bsb = jnp.broadcast_to(b_s[:, :, None], (heads, CHUNK, HEAD_DIM)).astype(F32)
    return pl.pallas_call(
        functools.partial(_even_a_kernel, tm=tm, heads=heads),
        grid=(t // tm,),
        in_specs=[pl.BlockSpec((tm, aw), lambda i: (i, 0)),
                  pl.BlockSpec((tm, aw), lambda i: (i, 1)),
                  pl.BlockSpec((1, aw), lambda i: (0, 0)),
                  pl.BlockSpec((heads, CHUNK, CHUNK), lambda i: (0, 0, 0)),
                  pl.BlockSpec((heads, CHUNK, HEAD_DIM), lambda i: (0, 0, 0))],
        out_specs=pl.BlockSpec((tm, aw), lambda i: (i, 0)),
        out_shape=jax.ShapeDtypeStruct((t, aw), BF16),
        compiler_params=_cp(("arbitrary",)), name="even_chunk_gmlp",
    )(p, p, g_v.reshape(1, aw), w_s.astype(BF16), bsb)


def _even_b1_kernel(z_ref, cs_ref, o_ref, *, groups):
    for g in range(groups):
        cs = slice(g * HEAD_DIM, (g + 1) * HEAD_DIM)
        r = jnp.dot(z_ref[:, cs], cs_ref[...], preferred_element_type=F32)
        o_ref[0, 0, :, cs] = r[:, :HEAD_DIM].astype(o_ref.dtype)
        o_ref[0, 1, :, cs] = r[:, HEAD_DIM:].astype(o_ref.dtype)


def even_b_feature_dft(p, cs_mat, row0, n, nb, tm):
    bw = p.shape[1] // 3
    groups = bw // HEAD_DIM
    tpb = n // tm
    t0 = row0 // tm
    return pl.pallas_call(
        functools.partial(_even_b1_kernel, groups=groups),
        grid=(nb, tpb),
        in_specs=[pl.BlockSpec((tm, bw), lambda b, j: (t0 + b * tpb + j, 2)),
                  pl.BlockSpec((HEAD_DIM, 2 * HEAD_DIM), lambda b, j: (0, 0))],
        out_specs=pl.BlockSpec((1, 2, tm, bw), lambda b, j: (b, 0, j, 0)),
        out_shape=jax.ShapeDtypeStruct((nb, 2, n, bw), BF16),
        compiler_params=_cp(("arbitrary", "arbitrary")), name="dft_feature",
    )(p, cs_mat)


def _dft_mats(n):
    k = jnp.arange(n, dtype=jnp.int32)
    kn = (k[:, None] * k[None, :]) % n
    ang = kn.astype(F32) * (2.0 * math.pi / n)
    s = 1.0 / math.sqrt(n)
    return jnp.cos(ang) * s, jnp.sin(ang) * s


def even_b(p, n_lat, n_ctx, nb, tm):
    c_d, s_d = _dft_mats(HEAD_DIM)
    cs_mat = jnp.concatenate([c_d, s_d], axis=1).astype(BF16)
    t_all = p.shape[0]
    y = None
    for row0, n in ((0, n_lat), (nb * n_lat, n_ctx)):
        zz = even_b_feature_dft(p, cs_mat, row0, n, nb, min(tm, n))
        c_n, s_n = _dft_mats(n)
        fcat = jnp.concatenate([c_n, -s_n], axis=1).astype(BF16)
        y = shared_lhs_bmm(fcat, zz.reshape(nb, 2 * n, zz.shape[-1]), t_all, row0, prev=y)
    return y


def _rope_kernel(q_ref, k_ref, cos_ref, sin_ref, o_ref, *, nblk):
    c = cos_ref[...]
    s = sin_ref[...]
    for src, base in ((q_ref, 0), (k_ref, nblk)):
        for j in range(nblk):
            t = src[:, j * HEAD_DIM:(j + 1) * HEAD_DIM].astype(F32)
            r = t * c + pltpu.roll(t, HEAD_DIM // 2, 1) * s
            o_ref[:, (base + j) * HEAD_DIM:(base + j + 1) * HEAD_DIM] = r.astype(o_ref.dtype)


def rope_qk(p, seq, nb, tm):
    dw = p.shape[1] // 6
    pos = jnp.arange(seq)
    row = (pos // GRID_W).astype(F32)
    col = (pos % GRID_W).astype(F32)
    nf = HEAD_DIM // 4
    inv = ROPE_THETA ** (-jnp.arange(nf, dtype=F32) / nf)
    ang = jnp.concatenate([row[:, None] * inv, col[:, None] * inv], axis=-1)
    cos, sin = jnp.cos(ang), jnp.sin(ang)
    cos2 = jnp.concatenate([cos, cos], axis=-1)
    sin2 = jnp.concatenate([-sin, sin], axis=-1)
    tpb = seq // tm
    return pl.pallas_call(
        functools.partial(_rope_kernel, nblk=dw // HEAD_DIM),
        grid=(nb * tpb,),
        in_specs=[pl.BlockSpec((tm, dw), lambda i: (i, 3)),
                  pl.BlockSpec((tm, dw), lambda i: (i, 4)),
                  pl.BlockSpec((tm, HEAD_DIM), lambda i: (i % tpb, 0)),
                  pl.BlockSpec((tm, HEAD_DIM), lambda i: (i % tpb, 0))],
        out_specs=pl.BlockSpec((tm, 2 * dw), lambda i: (i, 0)),
        out_shape=jax.ShapeDtypeStruct((nb * seq, 2 * dw), BF16),
        compiler_params=_cp(("arbitrary",)), name="rope",
    )(p, p, cos2, sin2)


def _diff_attn_kernel(*refs, seg_lens, kc, lambda_init):
    nseg = len(seg_lens)
    q_ref = refs[0]
    lam_ref, g_ref = refs[1 + 2 * nseg], refs[2 + 2 * nseg]
    o_ref = refs[-1]
    lp = lam_ref[...]
    lam = (jnp.exp(jnp.sum(lp[0:1] * lp[1:2], axis=-1, keepdims=True))
           - jnp.exp(jnp.sum(lp[2:3] * lp[3:4], axis=-1, keepdims=True)) + lambda_init)
    tq = q_ref.shape[0]
    dv = o_ref.shape[1]
    qs = (q_ref[:, :HEAD_DIM], q_ref[:, HEAD_DIM:])
    m = [jnp.full((tq, 1), -jnp.inf, F32) for _ in range(2)]
    l = [jnp.zeros((tq, 1), F32) for _ in range(2)]
    acc = [jnp.zeros((tq, dv), F32) for _ in range(2)]
    for si in range(nseg):
        k_ref, v_ref = refs[1 + 2 * si], refs[2 + 2 * si]
        n = seg_lens[si]
        step = min(kc, n)
        for c0 in range(0, n, step):
            kblk = k_ref[c0:c0 + step, :]
            vblk = v_ref[c0:c0 + step, :]
            for t in range(2):
                s = lax.dot_general(qs[t], kblk[:, t * HEAD_DIM:(t + 1) * HEAD_DIM],
                                    (((1,), (1,)), ((), ())), preferred_element_type=F32) * ATTN_SCALE
                m_new = jnp.maximum(m[t], jnp.max(s, axis=-1, keepdims=True))
                alpha = jnp.exp(m[t] - m_new)
                pr = jnp.exp(s - m_new)
                l[t] = alpha * l[t] + jnp.sum(pr, axis=-1, keepdims=True)
                acc[t] = alpha * acc[t] + jnp.dot(pr.astype(BF16), vblk, preferred_element_type=F32)
                m[t] = m_new
    o = acc[0] / l[0] - lam * (acc[1] / l[1])
    o = o * lax.rsqrt(jnp.mean(o * o, axis=-1, keepdims=True) + EPS) * g_ref[...]
    o_ref[...] = (o * (1.0 - lambda_init)).astype(o_ref.dtype)


def diff_attention(q_arr, q_row0, q_col0, nq, segs, lam_p, g_sub, nb, heads, lambda_init, tq,
                   out_rows, out_row0, prev=None):
    dv = 2 * HEAD_DIM
    tq = min(tq, nq)
    qpb = nq // tq
    in_specs = [pl.BlockSpec((tq, dv), lambda b, h, i: (q_row0 // tq + b * qpb + i, q_col0 + h))]
    args = [q_arr]
    seg_lens = []
    for (k_arr, k_row0, k_col0, v_arr, v_row0, v_col0, n) in segs:
        in_specs.append(pl.BlockSpec((n, dv), functools.partial(
            lambda b, h, i, r0, c0, nn: (r0 // nn + b, c0 + h), r0=k_row0, c0=k_col0, nn=n)))
        in_specs.append(pl.BlockSpec((n, dv), functools.partial(
            lambda b, h, i, r0, c0, nn: (r0 // nn + b, c0 + h), r0=v_row0, c0=v_col0, nn=n)))
        args += [k_arr, v_arr]
        seg_lens.append(n)
    in_specs += [pl.BlockSpec((4, HEAD_DIM), lambda b, h, i: (0, 0)),
                 pl.BlockSpec((1, dv), lambda b, h, i: (0, 0))]
    args += [lam_p.astype(F32), g_sub.reshape(1, dv)]
    aliases = {}
    if prev is not None:
        in_specs.append(pl.BlockSpec(memory_space=pl.ANY))
        aliases = {len(args): 0}
        args.append(prev)
    o0 = out_row0 // tq
    return pl.pallas_call(
        functools.partial(_diff_attn_kernel, seg_lens=tuple(seg_lens), kc=1024, lambda_init=lambda_init),
        grid=(nb, heads, qpb),
        in_specs=in_specs,
        out_specs=pl.BlockSpec((tq, dv), lambda b, h, i: (o0 + b * qpb + i, h)),
        out_shape=jax.ShapeDtypeStruct((out_rows, heads * dv), BF16),
        input_output_aliases=aliases,
        compiler_params=_cp(("arbitrary",) * 3), name="diff_attention",
    )(*args)


def _na_geometry(rows):
    kh = min(NA_KH, rows)
    qr = math.gcd(rows, NA_QR)
    kb = min(kh + qr - 1, rows)
    n_rb = rows // qr
    band_r = np.clip(np.arange(n_rb) * qr - kh // 2, 0, rows - kb)
    return kh, qr, kb, n_rb, band_r


def _na_bias(rpb, rows):
    kh, qr, kb, n_rb, band_r = _na_geometry(rows)
    nh = rpb.shape[0]
    rel = band_r - np.arange(n_rb) * qr
    cases, case_of = [], []
    for i in range(n_rb):
        q_row = i * qr + np.arange(qr)
        win_r = np.clip(q_row - kh // 2, 0, rows - kh)
        key = (int(rel[i]), tuple((win_r - i * qr).tolist()))
        if key not in cases:
            cases.append(key)
        case_of.append(cases.index(key))
    q_col = np.arange(GRID_W)
    key_c = np.arange(GRID_W)
    win_c = np.clip(q_col - NA_KW // 2, 0, GRID_W - NA_KW)
    ok_c = (key_c[None, :] >= win_c[:, None]) & (key_c[None, :] < win_c[:, None] + NA_KW)
    dc = np.clip(key_c[None, :] - q_col[:, None], 1 - NA_KW, NA_KW - 1) + NA_KW - 1
    onehot = np.zeros((2 * NA_KW - 1, GRID_W, GRID_W), np.float32)
    onehot[dc, q_col[:, None], key_c[None, :]] = 1.0
    col_tab = jnp.einsum("hab,bck->hack", rpb.astype(F32), jnp.asarray(onehot),
                         precision=lax.Precision.HIGHEST)
    col_tab = jnp.where(jnp.asarray(ok_c)[None, None], col_tab, NEG)
    tabs = []
    for (r, win_rel) in cases:
        q_row = np.arange(qr)
        key_r = r + np.arange(kb)
        win_r = np.asarray(win_rel)
        ok_r = (key_r[None, :] >= win_r[:, None]) & (key_r[None, :] < win_r[:, None] + kh)
        dr = np.clip(key_r[None, :] - q_row[:, None], 1 - NA_KH, NA_KH - 1) + NA_KH - 1
        blk = col_tab[:, dr.reshape(-1)].reshape(nh, qr, kb, GRID_W, GRID_W)
        blk = jnp.where(jnp.asarray(ok_r)[None, :, :, None, None], blk, NEG)
        tabs.append(blk.transpose(0, 1, 3, 2, 4).reshape(nh, qr * GRID_W, kb * GRID_W))
    return jnp.stack(tabs, axis=1), np.asarray(case_of, np.int32), band_r.astype(np.int32)


def _na_kernel(case_ref, start_ref, q_ref, k_ref, v_ref, kx_ref, vx_ref, bias_ref, o_ref, *, nkeys):
    i = pl.program_id(2)
    start = pl.multiple_of(start_ref[i], GRID_W)
    q = q_ref[...]
    kb = k_ref[pl.ds(start, nkeys), :]
    vb = v_ref[pl.ds(start, nkeys), :]
    dn = (((1,), (1,)), ((), ()))
    s_loc = lax.dot_general(q, kb, dn, preferred_element_type=F32) * ATTN_SCALE + bias_ref[0, 0]
    s_ctx = lax.dot_general(q, kx_ref[...], dn, preferred_element_type=F32) * ATTN_SCALE
    m = jnp.maximum(jnp.max(s_loc, axis=-1, keepdims=True), jnp.max(s_ctx, axis=-1, keepdims=True))
    p_loc = jnp.exp(s_loc - m)
    p_ctx = jnp.exp(s_ctx - m)
    den = jnp.sum(p_loc, axis=-1, keepdims=True) + jnp.sum(p_ctx, axis=-1, keepdims=True)
    o = (jnp.dot(p_loc.astype(BF16), vb, preferred_element_type=F32)
         + jnp.dot(p_ctx.astype(BF16), vx_ref[...], preferred_element_type=F32))
    o_ref[...] = (o / den).astype(o_ref.dtype)


def neighbourhood_attention(p, rpb, seq, n_ctx, nb, heads, out_rows):
    rows = seq // GRID_W
    kh, qr, kb, n_rb, _ = _na_geometry(rows)
    bias, case_of, band_r = _na_bias(rpb, rows)
    tq = qr * GRID_W
    nkeys = kb * GRID_W
    qpb = seq // tq
    xb0 = nb * seq // n_ctx
    grid_spec = pltpu.PrefetchScalarGridSpec(
        num_scalar_prefetch=2,
        grid=(nb, heads, n_rb),
        in_specs=[
            pl.BlockSpec((tq, HEAD_DIM), lambda b, h, i, cs, st: (b * qpb + i, h)),
            pl.BlockSpec((seq, HEAD_DIM), lambda b, h, i, cs, st: (b, heads + h)),
            pl.BlockSpec((seq, HEAD_DIM), lambda b, h, i, cs, st: (b, 2 * heads + h)),
            pl.BlockSpec((n_ctx, HEAD_DIM), lambda b, h, i, cs, st: (xb0 + b, heads + h)),
            pl.BlockSpec((n_ctx, HEAD_DIM), lambda b, h, i, cs, st: (xb0 + b, 2 * heads + h)),
            pl.BlockSpec((1, 1, tq, nkeys), lambda b, h, i, cs, st: (h, cs[i], 0, 0)),
        ],
        out_specs=pl.BlockSpec((tq, HEAD_DIM), lambda b, h, i, cs, st: (b * qpb + i, h)),
    )
    return pl.pallas_call(
        functools.partial(_na_kernel, nkeys=nkeys),
        grid_spec=grid_spec,
        out_shape=jax.ShapeDtypeStruct((out_rows, heads * HEAD_DIM), BF16),
        compiler_params=_cp(("arbitrary",) * 3), name="neighbourhood_attention",
    )(jnp.asarray(case_of), jnp.asarray(band_r * GRID_W), p, p, p, p, p, bias)


def _ctx_attn_kernel(q_ref, k_ref, v_ref, prev_ref, o_ref):
    del prev_ref
    s = lax.dot_general(q_ref[...], k_ref[...], (((1,), (1,)), ((), ())),
                        preferred_element_type=F32) * ATTN_SCALE
    m = jnp.max(s, axis=-1, keepdims=True)
    pr = jnp.exp(s - m)
    den = jnp.sum(pr, axis=-1, keepdims=True)
    o = jnp.dot(pr.astype(BF16), v_ref[...], preferred_element_type=F32)
    o_ref[...] = (o / den).astype(o_ref.dtype)


def ctx_dense_attention(p, row0, n_ctx, nb, heads, prev):
    xb0 = row0 // n_ctx
    return pl.pallas_call(
        _ctx_attn_kernel,
        grid=(nb, heads),
        in_specs=[pl.BlockSpec((n_ctx, HEAD_DIM), lambda b, h: (xb0 + b, h)),
                  pl.BlockSpec((n_ctx, HEAD_DIM), lambda b, h: (xb0 + b, heads + h)),
                  pl.BlockSpec((n_ctx, HEAD_DIM), lambda b, h: (xb0 + b, 2 * heads + h)),
                  pl.BlockSpec(memory_space=pl.ANY)],
        out_specs=pl.BlockSpec((n_ctx, HEAD_DIM), lambda b, h: (xb0 + b, h)),
        out_shape=jax.ShapeDtypeStruct(prev.shape, BF16),
        input_output_aliases={3: 0},
        compiler_params=_cp(("arbitrary", "arbitrary")), name="ctx_dense_attention",
    )(p, p, p, prev)


MOE_TM = 256


TOK_BITS = 14


def _expert_kernel(te_ref, tv_ref, code_ref, h_hbm, wgu_ref, bgu_ref, wdn_ref, bdn_ref, y_hbm,
                   xbuf, ybuf, gsem, ssem, *, de, tm, nt):
    i = pl.program_id(0)
    slot = i % 2
    nxt = jnp.minimum(i + 1, nt - 1)
    has_next = jnp.logical_and(i + 1 < nt, tv_ref[nxt] != 0)

    def gather_copy(tile, sl, r):
        tok = code_ref[tile * tm + r] & ((1 << TOK_BITS) - 1)
        return pltpu.make_async_copy(h_hbm.at[pl.ds(tok, 1)], xbuf.at[sl, pl.ds(r, 1)], gsem.at[sl])

    def scatter_copy(tile, sl, r):
        dst = lax.shift_right_logical(code_ref[tile * tm + r], TOK_BITS)
        return pltpu.make_async_copy(ybuf.at[sl, pl.ds(r, 1)], y_hbm.at[pl.ds(dst, 1)], ssem.at[sl])

    def start_all(copy_fn, tile, sl):
        def body(r, carry):
            copy_fn(tile, sl, r).start()
            return carry
        lax.fori_loop(0, tm, body, 0, unroll=8)

    def wait_all(buf, sem, sl):
        pltpu.make_async_copy(buf.at[sl], buf.at[sl], sem.at[sl]).wait()

    @pl.when(jnp.logical_and(i == 0, tv_ref[0] != 0))
    def _():
        start_all(gather_copy, 0, 0)

    @pl.when(tv_ref[i] != 0)
    def _():
        wait_all(xbuf, gsem, slot)

        @pl.when(has_next)
        def _():
            start_all(gather_copy, i + 1, 1 - slot)

        lo, hi = _unpack_halves(xbuf[slot])
        half = lo.shape[1]
        gu = (jnp.dot(lo.astype(BF16), wgu_ref[0, 0, :half, :], preferred_element_type=F32)
              + jnp.dot(hi.astype(BF16), wgu_ref[0, 0, half:, :], preferred_element_type=F32) + bgu_ref[0, 0])
        gate = jnp.minimum(gu[:, :de], SWIGLU_LIMIT)
        up = jnp.clip(gu[:, de:], -SWIGLU_LIMIT, SWIGLU_LIMIT)
        act = gate * _sigmoid(SWIGLU_ALPHA * gate) * (up + 1.0)
        y = jnp.dot(act.astype(BF16), wdn_ref[0, 0], preferred_element_type=F32) + bdn_ref[0, 0]

        @pl.when(i > 0)
        def _():
            wait_all(ybuf, ssem, 1 - slot)

        ybuf[slot] = _pack_halves(y.astype(BF16).astype(F32))
        start_all(scatter_copy, i, slot)

        @pl.when(jnp.logical_not(has_next))
        def _():
            wait_all(ybuf, ssem, slot)


def expert_ffn(h_packed, code, tile_e, tile_valid, w_gu, b_gu, w_dn, b_dn, layer, n_slots):
    t, dh = h_packed.shape
    d = 2 * dh
    nl, ne, _, f2 = w_gu.shape
    de = f2 // 2
    tm = MOE_TM
    nt = code.shape[0] // tm
    grid_spec = pltpu.PrefetchScalarGridSpec(
        num_scalar_prefetch=3,
        grid=(nt,),
        in_specs=[
            pl.BlockSpec(memory_space=pl.ANY),
            pl.BlockSpec((1, 1, d, f2), lambda i, te, tv, cd: (layer, te[i], 0, 0)),
            pl.BlockSpec((1, 1, 1, f2), lambda i, te, tv, cd: (layer, te[i], 0, 0)),
            pl.BlockSpec((1, 1, de, d), lambda i, te, tv, cd: (layer, te[i], 0, 0)),
            pl.BlockSpec((1, 1, 1, d), lambda i, te, tv, cd: (layer, te[i], 0, 0)),
        ],
        out_specs=pl.BlockSpec(memory_space=pl.ANY),
        scratch_shapes=[pltpu.VMEM((2, tm, dh), jnp.uint32), pltpu.VMEM((2, tm, dh), jnp.uint32),
                        pltpu.SemaphoreType.DMA((2,)), pltpu.SemaphoreType.DMA((2,))],
    )
    return pl.pallas_call(
        functools.partial(_expert_kernel, de=de, tm=tm, nt=nt),
        grid_spec=grid_spec,
        out_shape=jax.ShapeDtypeStruct((n_slots + tm, dh), jnp.uint32),
        compiler_params=_cp(("arbitrary",)), name="expert_ffn",
    )(tile_e, tile_valid, code, h_packed, w_gu, b_gu.reshape(nl, ne, 1, f2), w_dn, b_dn.reshape(nl, ne, 1, d))


def _combine_kernel(x_ref, y0_ref, y1_ref, y2_ref, y3_ref, wt_ref, g_ref, *rest, final_norm):
    o_ref = rest[-1]
    wt = wt_ref[...]
    half = y0_ref.shape[1]
    acc_lo = jnp.zeros((x_ref.shape[0], half), F32)
    acc_hi = jnp.zeros((x_ref.shape[0], half), F32)
    for k, y_ref in enumerate((y0_ref, y1_ref, y2_ref, y3_ref)):
        lo, hi = _unpack_halves(y_ref[...])
        acc_lo = acc_lo + wt[:, k:k + 1] * lo
        acc_hi = acc_hi + wt[:, k:k + 1] * hi
    x_lo = x_ref[:, :half] + g_ref[0, :, :half] * acc_lo
    x_hi = x_ref[:, half:] + g_ref[0, :, half:] * acc_hi
    if final_norm:
        nf_ref = rest[0]
        ms = (jnp.sum(x_lo * x_lo, axis=-1, keepdims=True)
              + jnp.sum(x_hi * x_hi, axis=-1, keepdims=True)) / (2 * half)
        inv = lax.rsqrt(ms + EPS)
        x_lo = x_lo * inv * nf_ref[:, :half]
        x_hi = x_hi * inv * nf_ref[:, half:]
    o_ref[:, :half] = x_lo
    o_ref[:, half:] = x_hi


def moe_combine(x, y4, wts, g3, seg, tm, norm_final=None):
    t, d = x.shape
    tiles = t // tm
    in_specs = [pl.BlockSpec((tm, d), lambda i: (i, 0))]
    in_specs += [pl.BlockSpec((tm, d // 2), functools.partial(lambda i, k: (k * tiles + i, 0), k=k))
                 for k in range(TOP_K)]
    in_specs += [pl.BlockSpec((tm, LANES), lambda i: (i, 0)),
                 pl.BlockSpec((1, 1, d), lambda i: (seg(i), 0, 0))]
    args = [x, y4, y4, y4, y4, wts, g3]
    if norm_final is not None:
        in_specs.append(pl.BlockSpec((1, d), lambda i: (0, 0)))
        args.append(norm_final.reshape(1, d))
    return pl.pallas_call(
        functools.partial(_combine_kernel, final_norm=norm_final is not None),
        grid=(tiles,), in_specs=in_specs,
        out_specs=pl.BlockSpec((tm, d), lambda i: (i, 0)),
        out_shape=jax.ShapeDtypeStruct((t, d), F32),
        compiler_params=_cp(("arbitrary",)), name="moe_combine",
    )(*args)


def moe_layer(x, g_ffn, sc3, sh3, g3, seg, tm, w_r, b_r, w_gu, b_gu, w_dn, b_dn, layer, norm_final=None):
    t, d = x.shape
    h, topi, wts, rk, cnt = modulate(x, g_ffn, sc3, sh3, seg, tm, None, w_r, b_r)
    tme = MOE_TM
    ns = t * TOP_K
    nt = -(-(ns + N_EXPERTS * (tme - 1)) // tme)
    counts = cnt[0, :N_EXPERTS]
    padded = ((counts + tme - 1) // tme) * tme
    ends = jnp.cumsum(padded)
    offs = ends - padded
    ef = topi[:, :TOP_K]
    onehot = ef[:, :, None] == jnp.arange(N_EXPERTS, dtype=jnp.int32)[None, None, :]
    pos = rk[:, :TOP_K] + jnp.sum(jnp.where(onehot, offs[None, None, :], 0), axis=-1)
    assert t <= (1 << TOK_BITS) and ns + tme < (1 << (31 - TOK_BITS))
    tok = jnp.arange(t, dtype=jnp.int32)[:, None]
    slot_code = tok | ((jnp.arange(TOP_K, dtype=jnp.int32)[None, :] * t + tok) << TOK_BITS)
    pad_code = (ns + (jnp.arange(nt * tme, dtype=jnp.int32) % tme)) << TOK_BITS
    code = pad_code.at[pos.reshape(ns)].set(slot_code.reshape(ns))
    tile_start = jnp.arange(nt, dtype=jnp.int32) * tme
    tile_valid = (tile_start < ends[-1]).astype(jnp.int32)
    tile_e = jnp.minimum(jnp.searchsorted(ends, tile_start, side="right").astype(jnp.int32), N_EXPERTS - 1)
    tile_e = jnp.where(tile_valid != 0, tile_e, jnp.max(jnp.where(tile_valid != 0, tile_e, 0)))
    y4 = expert_ffn(h, code, tile_e, tile_valid, w_gu, b_gu, w_dn, b_dn, layer, ns)
    return moe_combine(x, y4, wts, g3, seg, tm, norm_final)


def kernel(x, c, ctx, c_ctx, ada_w1, ada_w2, ada_b, norm_mix, norm_ffn, ev_w_in, ev_v_norm, ev_w_s, ev_b_s, ev_w_out, od_w_in, od_rpb, od_lam, od_sub_norm, od_w_out, moe_w_r, moe_b_r, moe_w_gu, moe_b_gu, moe_w_dn, moe_b_dn, norm_final):
    nb, seq, d = x.shape
    n_ctx = ctx.shape[1]
    depth = ada_w1.shape[0]
    n_lat = nb * seq
    t_all = n_lat + nb * n_ctx
    tm = 512 if (t_all % 512 == 0 and seq % 512 == 0) else 256
    tms = 256
    assert n_lat % tm == 0 and (nb * n_ctx) % tms == 0 and seq % tms == 0 and nb + 1 <= 8

    def seg_for(tile):
        return _seg_fn(n_lat // tile, seq // tile, nb)

    ev_w_in_b = cast_bf16(ev_w_in, 256)
    ev_w_out_b = cast_bf16(ev_w_out, 512)
    od_w_in_b = cast_bf16(od_w_in, 128)
    od_w_out_b = cast_bf16(od_w_out, 512)
    w_gu_b = cast_bf16(moe_w_gu, 2048)
    w_dn_b = cast_bf16(moe_w_dn, moe_w_dn.shape[2])

    xs = jnp.concatenate([x.reshape(n_lat, d), ctx.reshape(nb * n_ctx, d)], axis=0)

    cv = jnp.zeros((8, d), F32).at[:nb].set(c).at[nb].set(c_ctx)
    mods = ada_all(cv, ada_w1, ada_w2, ada_b)

    for i in range(depth):
        last = i == depth - 1
        j = i // 2
        m6 = mods[i].reshape(8, 6, d)
        sh1, sc1, g1, sh2, sc2, g2 = [m6[:, q, :].reshape(8, 1, d) for q in range(6)]
        h = modulate(xs, norm_mix[i], sc1, sh1, seg_for(tms), tms)
        n_out = n_lat if last else t_all
        if i % 2 == 0:
            p = matmul(h, ev_w_in_b, j, tm)
            y1 = even_a(p, ev_v_norm[j], ev_w_s[j], ev_b_s[j], tms)
            y2 = even_b(p, seq, n_ctx, nb, tms)
            w_out = ev_w_out_b
        else:
            lambda_init = 0.8 - 0.6 * math.exp(-0.3 * i)
            p = matmul(h, od_w_in_b, j, tm)
            cw = p.shape[1] // 6
            heads_c = cw // HEAD_DIM
            heads_d = cw // (2 * HEAD_DIM)
            y1 = neighbourhood_attention(p, od_rpb[j], seq, n_ctx, nb, heads_c, n_out)
            qk = rope_qk(p, seq, nb, tms)
            nblk = cw // (2 * HEAD_DIM)
            seg_x = (p, n_lat, 4 * nblk, p, n_lat, 5 * nblk, n_ctx)
            seg_l = (qk, 0, nblk, p, 0, 5 * nblk, seq)
            y2 = diff_attention(qk, 0, 0, seq, [seg_x, seg_l], od_lam[j], od_sub_norm[j], nb, heads_d,
                                lambda_init, 256, n_out, 0)
            if not last:
                y1 = ctx_dense_attention(p, n_lat, n_ctx, nb, heads_c, y1)
                y2 = diff_attention(p, n_lat, 3 * nblk, n_ctx, [seg_x], od_lam[j], od_sub_norm[j], nb,
                                    heads_d, lambda_init, 256, n_out, n_lat, prev=y2)
            w_out = od_w_out_b
        xs = matmul_residual(y1, y2, w_out, j, xs, g1, seg_for(tm), tm, n_out)
        xs = moe_layer(xs, norm_ffn[i], sc2, sh2, g2, seg_for(tms), tms, moe_w_r[i], moe_b_r[i],
                       w_gu_b, moe_b_gu, w_dn_b, moe_b_dn, i, norm_final if last else None)
    return xs.reshape(nb, seq, d)
```

```python
import functools
import math

import numpy as np
import jax
import jax.numpy as jnp
from jax import lax
from jax.experimental import pallas as pl
from jax.experimental.pallas import tpu as pltpu

F32 = jnp.float32
BF16 = jnp.bfloat16

GRID_W = 64
HEAD_DIM = 128
EPS = 1e-6
ROPE_THETA = 10000.0
CHUNK = 128
NA_KH = 8
NA_KW = 16
NA_QR = 8
ATTN_SCALE = HEAD_DIM ** -0.5
NEG = -1e30
N_EXPERTS = 32
TOP_K = 4
SWIGLU_LIMIT = 7.0
SWIGLU_ALPHA = 1.702
LANES = 128

VMEM_LIMIT = 56 * 1024 * 1024


def _cp(sem, vmem=VMEM_LIMIT):
    return pltpu.CompilerParams(dimension_semantics=sem, vmem_limit_bytes=vmem)


def _gelu_tanh(x):
    return 0.5 * x * (1.0 + jnp.tanh(0.7978845608028654 * (x + 0.044715 * (x * x * x))))


def _sigmoid(x):
    return 1.0 / (1.0 + jnp.exp(-x))


def _pack_halves(x):
    c = x.shape[1] // 2
    return _pack_pair(x[:, :c], x[:, c:])


def _pack_pair(left, right):
    lo = lax.shift_right_logical(lax.bitcast_convert_type(left, jnp.uint32), jnp.uint32(16))
    hi = lax.bitcast_convert_type(right, jnp.uint32) & jnp.uint32(0xFFFF0000)
    return lo | hi


def _unpack_halves(w):
    lo = lax.bitcast_convert_type(lax.shift_left(w, jnp.uint32(16)), F32)
    hi = lax.bitcast_convert_type(w & jnp.uint32(0xFFFF0000), F32)
    return lo, hi


def _pick(n, prefs):
    for p in prefs:
        if n % p == 0:
            return p
    raise ValueError(f"no tile for {n}")


def _ada_kernel(cv_ref, w1_ref, w2_ref, b_ref, o_ref):
    cv = cv_ref[...]
    s = cv * _sigmoid(cv)
    t = jnp.dot(s, w1_ref[0], preferred_element_type=F32, precision=lax.Precision.HIGHEST)
    m = jnp.dot(t, w2_ref[0], preferred_element_type=F32, precision=lax.Precision.HIGHEST)
    o_ref[0] = m + b_ref[0]


def ada_all(cv, w1, w2, b):
    nl, d, r = w1.shape
    n6 = w2.shape[2]
    tn = 2048
    return pl.pallas_call(
        _ada_kernel,
        grid=(nl, n6 // tn),
        in_specs=[
            pl.BlockSpec((8, d), lambda l, j: (0, 0)),
            pl.BlockSpec((1, d, r), lambda l, j: (l, 0, 0)),
            pl.BlockSpec((1, r, tn), lambda l, j: (l, 0, j)),
            pl.BlockSpec((1, 1, tn), lambda l, j: (l, 0, j)),
        ],
        out_specs=pl.BlockSpec((1, 8, tn), lambda l, j: (l, 0, j)),
        out_shape=jax.ShapeDtypeStruct((nl, 8, n6), F32),
        compiler_params=_cp(("arbitrary", "arbitrary")),
        name="ada_mod",
    )(cv, w1, w2, b.reshape(nl, 1, n6))


def _seg_fn(n_lat_tiles, tiles_per_batch, n_batch):
    def seg(i):
        return jnp.where(i < n_lat_tiles, i // tiles_per_batch, n_batch)
    return seg


def _mod_kernel(x_ref, g_ref, sc_ref, sh_ref, o_ref):
    x = x_ref[...]
    ms = jnp.mean(x * x, axis=-1, keepdims=True)
    y = x * lax.rsqrt(ms + EPS) * g_ref[...]
    o_ref[...] = (y * (1.0 + sc_ref[0]) + sh_ref[0]).astype(o_ref.dtype)


def _mod_router_kernel(x_ref, g_ref, sc_ref, sh_ref, wr_ref, br_ref, o_ref, ti_ref, wt_ref, rk_ref, cnt_ref,
                       base_ref):
    @pl.when(pl.program_id(0) == 0)
    def _():
        base_ref[...] = jnp.zeros_like(base_ref)

    x = x_ref[...]
    ms = jnp.mean(x * x, axis=-1, keepdims=True)
    y = x * lax.rsqrt(ms + EPS) * g_ref[...]
    h = y * (1.0 + sc_ref[0]) + sh_ref[0]
    h_hi = h.astype(BF16)
    o_ref[...] = _pack_halves(h_hi.astype(F32))
    h_lo = (h - h_hi.astype(F32)).astype(BF16)
    w = wr_ref[...]
    w_hi = w.astype(BF16)
    w_lo = (w - w_hi.astype(F32)).astype(BF16)
    logits = (jnp.dot(h_hi, w_hi, preferred_element_type=F32)
              + jnp.dot(h_hi, w_lo, preferred_element_type=F32)
              + jnp.dot(h_lo, w_hi, preferred_element_type=F32)) + br_ref[...]
    tm = logits.shape[0]
    lane = lax.broadcasted_iota(jnp.int32, logits.shape, 1).astype(F32)
    vals, idxs = [], []
    cur = logits
    for _ in range(TOP_K):
        m = jnp.max(cur, axis=-1, keepdims=True)
        idx = jnp.min(jnp.where(cur == m, lane, float(LANES)), axis=-1, keepdims=True)
        vals.append(m)
        idxs.append(idx)
        cur = jnp.where(lane == idx, -jnp.inf, cur)
    es = [jnp.exp(v - vals[0]) for v in vals]
    den = es[0] + es[1] + es[2] + es[3]
    sel = [lane == idxs[k] for k in range(TOP_K)]
    osum = jnp.zeros(logits.shape, F32)
    for k in range(TOP_K):
        osum = osum + jnp.where(sel[k], 1.0, 0.0)
    r_i = lax.broadcasted_iota(jnp.int32, (tm, tm), 0)
    c_i = lax.broadcasted_iota(jnp.int32, (tm, tm), 1)
    lower = jnp.where(c_i < r_i, 1.0, 0.0).astype(BF16)
    pref = jnp.dot(lower, osum.astype(BF16), preferred_element_type=F32) + base_ref[...]
    ti = jnp.zeros(logits.shape, F32)
    wt = jnp.zeros(logits.shape, F32)
    rk = jnp.zeros(logits.shape, F32)
    for k in range(TOP_K):
        rank_k = jnp.sum(jnp.where(sel[k], pref, 0.0), axis=-1, keepdims=True)
        ti = jnp.where(lane == float(k), idxs[k], ti)
        wt = jnp.where(lane == float(k), es[k] / den, wt)
        rk = jnp.where(lane == float(k), rank_k, rk)
    ti_ref[...] = ti.astype(jnp.int32)
    wt_ref[...] = wt
    rk_ref[...] = rk.astype(jnp.int32)
    base_ref[...] += jnp.sum(osum, axis=0, keepdims=True)
    cnt_ref[...] = base_ref[...].astype(jnp.int32)


def modulate(x, g, sc3, sh3, seg, tm, m_rows=None, w_r=None, b_r=None):
    t, d = x.shape
    t = t if m_rows is None else m_rows
    in_specs = [
        pl.BlockSpec((tm, d), lambda i: (i, 0)),
        pl.BlockSpec((1, d), lambda i: (0, 0)),
        pl.BlockSpec((1, 1, d), lambda i: (seg(i), 0, 0)),
        pl.BlockSpec((1, 1, d), lambda i: (seg(i), 0, 0)),
    ]
    if w_r is None:
        return pl.pallas_call(
            _mod_kernel, grid=(t // tm,), in_specs=in_specs,
            out_specs=pl.BlockSpec((tm, d), lambda i: (i, 0)),
            out_shape=jax.ShapeDtypeStruct((t, d), BF16),
            compiler_params=_cp(("arbitrary",)), name="modulate",
        )(x, g.reshape(1, d), sc3, sh3)
    wr_pad = jnp.zeros((d, LANES), F32).at[:, :N_EXPERTS].set(w_r)
    br_pad = jnp.full((1, LANES), NEG, F32).at[0, :N_EXPERTS].set(b_r)
    in_specs += [pl.BlockSpec((d, LANES), lambda i: (0, 0)), pl.BlockSpec((1, LANES), lambda i: (0, 0))]
    row_spec = pl.BlockSpec((tm, LANES), lambda i: (i, 0))
    return pl.pallas_call(
        _mod_router_kernel, grid=(t // tm,), in_specs=in_specs,
        out_specs=[pl.BlockSpec((tm, d // 2), lambda i: (i, 0)), row_spec, row_spec, row_spec,
                   pl.BlockSpec((1, LANES), lambda i: (0, 0))],
        out_shape=[jax.ShapeDtypeStruct((t, d // 2), jnp.uint32),
                   jax.ShapeDtypeStruct((t, LANES), jnp.int32),
                   jax.ShapeDtypeStruct((t, LANES), F32),
                   jax.ShapeDtypeStruct((t, LANES), jnp.int32),
                   jax.ShapeDtypeStruct((1, LANES), jnp.int32)],
        scratch_shapes=[pltpu.VMEM((1, LANES), F32)],
        compiler_params=_cp(("arbitrary",)), name="modulate_router",
    )(x, g.reshape(1, d), sc3, sh3, wr_pad, br_pad)


def _cast_kernel(x_ref, o_ref):
    o_ref[...] = x_ref[...].astype(o_ref.dtype)


def cast_bf16(w, block_rows):
    shape = w.shape
    w2 = w.reshape(-1, shape[-1])
    r, c = w2.shape
    out = pl.pallas_call(
        _cast_kernel, grid=(r // block_rows,),
        in_specs=[pl.BlockSpec((block_rows, c), lambda i: (i, 0))],
        out_specs=pl.BlockSpec((block_rows, c), lambda i: (i, 0)),
        out_shape=jax.ShapeDtypeStruct((r, c), BF16),
        compiler_params=_cp(("arbitrary",)), name="cast_bf16",
    )(w2)
    return out.reshape(shape)


def _mm_kernel(a_ref, b_ref, o_ref):
    o_ref[...] = jnp.dot(a_ref[...], b_ref[0], preferred_element_type=F32).astype(o_ref.dtype)


def matmul(a, b3, layer, tm, out_dtype=BF16, tn=1024):
    m, kk = a.shape
    n = b3.shape[2]
    tn = _pick(n, (tn, 512, 256))
    return pl.pallas_call(
        _mm_kernel,
        grid=(n // tn, m // tm),
        in_specs=[pl.BlockSpec((tm, kk), lambda j, i: (i, 0)),
                  pl.BlockSpec((1, kk, tn), lambda j, i: (layer, 0, j))],
        out_specs=pl.BlockSpec((tm, tn), lambda j, i: (i, j)),
        out_shape=jax.ShapeDtypeStruct((m, n), out_dtype),
        compiler_params=_cp(("arbitrary", "arbitrary")), name="matmul",
    )(a, b3)


def _mm_res_kernel(a1_ref, a2_ref, b_ref, x_ref, g_ref, o_ref):
    k1 = a1_ref.shape[1]
    acc = jnp.dot(a1_ref[...], b_ref[0, :k1, :], preferred_element_type=F32)
    acc = acc + jnp.dot(a2_ref[...], b_ref[0, k1:, :], preferred_element_type=F32)
    o_ref[...] = x_ref[...] + g_ref[0] * acc


def matmul_residual(a1, a2, b3, layer, x, g3, seg, tm, m_rows, tn=1024):
    k1, k2 = a1.shape[1], a2.shape[1]
    n = b3.shape[2]
    tn = _pick(n, (tn, 512, 256))
    return pl.pallas_call(
        _mm_res_kernel,
        grid=(n // tn, m_rows // tm),
        in_specs=[pl.BlockSpec((tm, k1), lambda j, i: (i, 0)),
                  pl.BlockSpec((tm, k2), lambda j, i: (i, 0)),
                  pl.BlockSpec((1, k1 + k2, tn), lambda j, i: (layer, 0, j)),
                  pl.BlockSpec((tm, tn), lambda j, i: (i, j)),
                  pl.BlockSpec((1, 1, tn), lambda j, i: (seg(i), 0, j))],
        out_specs=pl.BlockSpec((tm, tn), lambda j, i: (i, j)),
        out_shape=jax.ShapeDtypeStruct((m_rows, n), F32),
        compiler_params=_cp(("arbitrary", "arbitrary")), name="matmul_residual",
    )(a1, a2, b3, x, g3)


def _bmm_kernel(a_ref, b1_ref, b2_ref, *rest, nkh):
    o_ref, acc_ref = rest[-2], rest[-1]
    k = pl.program_id(3)

    @pl.when(k == 0)
    def _():
        acc_ref[...] = jnp.zeros_like(acc_ref)

    @pl.when(k < nkh)
    def _():
        acc_ref[...] += jnp.dot(a_ref[...], b1_ref[0], preferred_element_type=F32)

    @pl.when(k >= nkh)
    def _():
        acc_ref[...] += jnp.dot(a_ref[...], b2_ref[0], preferred_element_type=F32)

    @pl.when(k == 2 * nkh - 1)
    def _():
        o_ref[...] = acc_ref[...].astype(o_ref.dtype)


def shared_lhs_bmm(a, b1, b2, out_rows, row0, prev=None):
    m, kk2 = a.shape
    nb, kk, n = b1.shape
    tm, tn, tk = _pick(m, (512, 256, 128)), _pick(n, (1024, 512)), _pick(kk, (4096, 2048, 1024, 512, 256, 128))
    nkh = kk // tk
    nk = 2 * nkh
    r0 = row0 // tm
    mt = m // tm
    in_specs = [pl.BlockSpec((tm, tk), lambda z, j, i, k: (i, k)),
                pl.BlockSpec((1, tk, tn), lambda z, j, i, k: (z, jnp.minimum(k, nkh - 1), j)),
                pl.BlockSpec((1, tk, tn), lambda z, j, i, k: (z, jnp.maximum(k - nkh, 0), j))]
    args = [a, b1, b2]
    aliases = {}
    if prev is not None:
        in_specs.append(pl.BlockSpec(memory_space=pl.ANY))
        args.append(prev)
        aliases = {3: 0}
    return pl.pallas_call(
        functools.partial(_bmm_kernel, nkh=nkh),
        grid=(nb, n // tn, mt, nk),
        in_specs=in_specs,
        out_specs=pl.BlockSpec((tm, tn), lambda z, j, i, k: (r0 + z * mt + i, j)),
        out_shape=jax.ShapeDtypeStruct((out_rows, n), BF16),
        scratch_shapes=[pltpu.VMEM((tm, tn), F32)],
        input_output_aliases=aliases,
        compiler_params=_cp(("arbitrary",) * 4), name="dft_seq_matmul",
    )(*args)


def _even_a_kernel(u_ref, v_ref, gv_ref, ws_ref, bs_ref, o_ref, *, tm, heads):
    for h in range(heads):
        cs = slice(h * HEAD_DIM, (h + 1) * HEAD_DIM)
        u = _gelu_tanh(u_ref[:, cs].astype(F32))
        v = _gelu_tanh(v_ref[:, cs].astype(F32))
        v = v * lax.rsqrt(jnp.mean(v * v, axis=-1, keepdims=True) + EPS) * gv_ref[:, cs]
        vb = v.astype(BF16)
        w = ws_ref[h]
        bias = bs_ref[h]
        for c in range(tm // CHUNK):
            rs = slice(c * CHUNK, (c + 1) * CHUNK)
            mixed = jnp.dot(w, vb[rs], preferred_element_type=F32) + bias
            o_ref[rs, cs] = (u[rs] * mixed).astype(o_ref.dtype)


def even_a(p, g_v, w_s, b_s, tm):
    t = p.shape[0]
    heads = w_s.shape[0]
    aw = heads * HEAD_DIM
    bsb = jnp.broadcast_to(b_s[:, :, None], (heads, CHUNK, HEAD_DIM)).astype(F32)
    return pl.pallas_call(
        functools.partial(_even_a_kernel, tm=tm, heads=heads),
        grid=(t // tm,),
        in_specs=[pl.BlockSpec((tm, aw), lambda i: (i, 0)),
                  pl.BlockSpec((tm, aw), lambda i: (i, 1)),
                  pl.BlockSpec((1, aw), lambda i: (0, 0)),
                  pl.BlockSpec((heads, CHUNK, CHUNK), lambda i: (0, 0, 0)),
                  pl.BlockSpec((heads, CHUNK, HEAD_DIM), lambda i: (0, 0, 0))],
        out_specs=pl.BlockSpec((tm, aw), lambda i: (i, 0)),
        out_shape=jax.ShapeDtypeStruct((t, aw), BF16),
        compiler_params=_cp(("arbitrary",)), name="even_chunk_gmlp",
    )(p, p, g_v.reshape(1, aw), w_s.astype(BF16), bsb)


def _even_b1_kernel(z_ref, cs_ref, oc_ref, os_ref, *, groups):
    for g in range(groups):
        cs = slice(g * HEAD_DIM, (g + 1) * HEAD_DIM)
        r = jnp.dot(z_ref[:, cs], cs_ref[...], preferred_element_type=F32)
        oc_ref[0, :, cs] = r[:, :HEAD_DIM].astype(oc_ref.dtype)
        os_ref[0, :, cs] = r[:, HEAD_DIM:].astype(os_ref.dtype)


def even_b_feature_dft(p, cs_mat, row0, n, nb, tm):
    bw = p.shape[1] // 3
    groups = bw // HEAD_DIM
    tpb = n // tm
    t0 = row0 // tm
    out_spec = pl.BlockSpec((1, tm, bw), lambda b, j: (b, j, 0))
    return pl.pallas_call(
        functools.partial(_even_b1_kernel, groups=groups),
        grid=(nb, tpb),
        in_specs=[pl.BlockSpec((tm, bw), lambda b, j: (t0 + b * tpb + j, 2)),
                  pl.BlockSpec((HEAD_DIM, 2 * HEAD_DIM), lambda b, j: (0, 0))],
        out_specs=[out_spec, out_spec],
        out_shape=[jax.ShapeDtypeStruct((nb, n, bw), BF16), jax.ShapeDtypeStruct((nb, n, bw), BF16)],
        compiler_params=_cp(("arbitrary", "arbitrary")), name="dft_feature",
    )(p, cs_mat)


def _dft_mats(n):
    k = jnp.arange(n, dtype=jnp.int32)
    kn = (k[:, None] * k[None, :]) % n
    ang = kn.astype(F32) * (2.0 * math.pi / n)
    s = 1.0 / math.sqrt(n)
    return jnp.cos(ang) * s, jnp.sin(ang) * s


def even_b(p, n_lat, n_ctx, nb, tm):
    c_d, s_d = _dft_mats(HEAD_DIM)
    cs_mat = jnp.concatenate([c_d, s_d], axis=1).astype(BF16)
    t_all = p.shape[0]
    y = None
    for row0, n in ((0, n_lat), (nb * n_lat, n_ctx)):
        zc, zs = even_b_feature_dft(p, cs_mat, row0, n, nb, min(tm, n))
        c_n, s_n = _dft_mats(n)
        fcat = jnp.concatenate([c_n, -s_n], axis=1).astype(BF16)
        y = shared_lhs_bmm(fcat, zc, zs, t_all, row0, prev=y)
    return y


def _rope_kernel(q_ref, k_ref, cos_ref, sin_ref, o_ref, *, nblk):
    c = cos_ref[...]
    s = sin_ref[...]
    for src, base in ((q_ref, 0), (k_ref, nblk)):
        for j in range(nblk):
            t = src[:, j * HEAD_DIM:(j + 1) * HEAD_DIM].astype(F32)
            r = t * c + pltpu.roll(t, HEAD_DIM // 2, 1) * s
            o_ref[:, (base + j) * HEAD_DIM:(base + j + 1) * HEAD_DIM] = r.astype(o_ref.dtype)


def rope_qk(p, seq, nb, tm):
    dw = p.shape[1] // 6
    pos = jnp.arange(seq)
    row = (pos // GRID_W).astype(F32)
    col = (pos % GRID_W).astype(F32)
    nf = HEAD_DIM // 4
    inv = ROPE_THETA ** (-jnp.arange(nf, dtype=F32) / nf)
    ang = jnp.concatenate([row[:, None] * inv, col[:, None] * inv], axis=-1)
    cos, sin = jnp.cos(ang), jnp.sin(ang)
    cos2 = jnp.concatenate([cos, cos], axis=-1)
    sin2 = jnp.concatenate([-sin, sin], axis=-1)
    tpb = seq // tm
    return pl.pallas_call(
        functools.partial(_rope_kernel, nblk=dw // HEAD_DIM),
        grid=(nb * tpb,),
        in_specs=[pl.BlockSpec((tm, dw), lambda i: (i, 3)),
                  pl.BlockSpec((tm, dw), lambda i: (i, 4)),
                  pl.BlockSpec((tm, HEAD_DIM), lambda i: (i % tpb, 0)),
                  pl.BlockSpec((tm, HEAD_DIM), lambda i: (i % tpb, 0))],
        out_specs=pl.BlockSpec((tm, 2 * dw), lambda i: (i, 0)),
        out_shape=jax.ShapeDtypeStruct((nb * seq, 2 * dw), BF16),
        compiler_params=_cp(("arbitrary",)), name="rope",
    )(p, p, cos2, sin2)


def _diff_attn_kernel(*refs, seg_lens, kc, lambda_init):
    nseg = len(seg_lens)
    q_ref = refs[0]
    lam_ref, g_ref = refs[1 + 2 * nseg], refs[2 + 2 * nseg]
    o_ref = refs[-1]
    lp = lam_ref[...]
    lam = (jnp.exp(jnp.sum(lp[0:1] * lp[1:2], axis=-1, keepdims=True))
           - jnp.exp(jnp.sum(lp[2:3] * lp[3:4], axis=-1, keepdims=True)) + lambda_init)
    tq = q_ref.shape[0]
    dv = o_ref.shape[1]
    qs = (q_ref[:, :HEAD_DIM], q_ref[:, HEAD_DIM:])
    chunks = []
    for si in range(nseg):
        n = seg_lens[si]
        step = min(kc, n)
        chunks += [(refs[1 + 2 * si], refs[2 + 2 * si], c0, step) for c0 in range(0, n, step)]

    def scores(t, k_ref, c0, step):
        return lax.dot_general(qs[t], k_ref[c0:c0 + step, t * HEAD_DIM:(t + 1) * HEAD_DIM],
                               (((1,), (1,)), ((), ())), preferred_element_type=F32)

    def fold(x, op):
        r = x[:, :LANES]
        for j in range(1, x.shape[1] // LANES):
            r = op(r, x[:, j * LANES:(j + 1) * LANES])
        return r

    mx = [None, None]
    for (k_ref, _, c0, step) in chunks:
        for t in range(2):
            f = fold(scores(t, k_ref, c0, step), jnp.maximum)
            mx[t] = f if mx[t] is None else jnp.maximum(mx[t], f)
    c_exp = ATTN_SCALE * math.log2(math.e)
    mc = [jnp.max(mx[t], axis=-1, keepdims=True) * c_exp for t in range(2)]
    lsum = [None, None]
    acc = [None, None]
    for (k_ref, v_ref, c0, step) in chunks:
        vblk = v_ref[c0:c0 + step, :]
        for t in range(2):
            pr = jnp.exp2(scores(t, k_ref, c0, step) * c_exp - mc[t])
            f = fold(pr, jnp.add)
            lsum[t] = f if lsum[t] is None else lsum[t] + f
            pv = jnp.dot(pr.astype(BF16), vblk, preferred_element_type=F32)
            acc[t] = pv if acc[t] is None else acc[t] + pv
    l = [jnp.sum(lsum[t], axis=-1, keepdims=True) for t in range(2)]
    o = acc[0] / l[0] - lam * (acc[1] / l[1])
    o = o * lax.rsqrt(jnp.mean(o * o, axis=-1, keepdims=True) + EPS) * g_ref[...]
    o_ref[...] = (o * (1.0 - lambda_init)).astype(o_ref.dtype)


def diff_attention(q_arr, q_row0, q_col0, nq, segs, lam_p, g_sub, nb, heads, lambda_init, tq,
                   out_rows, out_row0, prev=None):
    dv = 2 * HEAD_DIM
    tq = min(tq, nq)
    qpb = nq // tq
    in_specs = [pl.BlockSpec((tq, dv), lambda b, h, i: (q_row0 // tq + b * qpb + i, q_col0 + h))]
    args = [q_arr]
    seg_lens = []
    for (k_arr, k_row0, k_col0, v_arr, v_row0, v_col0, n) in segs:
        in_specs.append(pl.BlockSpec((n, dv), functools.partial(
            lambda b, h, i, r0, c0, nn: (r0 // nn + b, c0 + h), r0=k_row0, c0=k_col0, nn=n)))
        in_specs.append(pl.BlockSpec((n, dv), functools.partial(
            lambda b, h, i, r0, c0, nn: (r0 // nn + b, c0 + h), r0=v_row0, c0=v_col0, nn=n)))
        args += [k_arr, v_arr]
        seg_lens.append(n)
    in_specs += [pl.BlockSpec((4, HEAD_DIM), lambda b, h, i: (0, 0)),
                 pl.BlockSpec((1, dv), lambda b, h, i: (0, 0))]
    args += [lam_p.astype(F32), g_sub.reshape(1, dv)]
    aliases = {}
    if prev is not None:
        in_specs.append(pl.BlockSpec(memory_space=pl.ANY))
        aliases = {len(args): 0}
        args.append(prev)
    o0 = out_row0 // tq
    return pl.pallas_call(
        functools.partial(_diff_attn_kernel, seg_lens=tuple(seg_lens), kc=1024, lambda_init=lambda_init),
        grid=(nb, heads, qpb),
        in_specs=in_specs,
        out_specs=pl.BlockSpec((tq, dv), lambda b, h, i: (o0 + b * qpb + i, h)),
        out_shape=jax.ShapeDtypeStruct((out_rows, heads * dv), BF16),
        input_output_aliases=aliases,
        compiler_params=_cp(("arbitrary",) * 3), name="diff_attention",
    )(*args)


def _na_geometry(rows):
    kh = min(NA_KH, rows)
    qr = math.gcd(rows, NA_QR)
    kb = min(kh + qr - 1, rows)
    n_rb = rows // qr
    band_r = np.clip(np.arange(n_rb) * qr - kh // 2, 0, rows - kb)
    return kh, qr, kb, n_rb, band_r


def _na_bias(rpb, rows):
    kh, qr, kb, n_rb, band_r = _na_geometry(rows)
    nh = rpb.shape[0]
    rel = band_r - np.arange(n_rb) * qr
    cases, case_of = [], []
    for i in range(n_rb):
        q_row = i * qr + np.arange(qr)
        win_r = np.clip(q_row - kh // 2, 0, rows - kh)
        key = (int(rel[i]), tuple((win_r - i * qr).tolist()))
        if key not in cases:
            cases.append(key)
        case_of.append(cases.index(key))
    q_col = np.arange(GRID_W)
    key_c = np.arange(GRID_W)
    win_c = np.clip(q_col - NA_KW // 2, 0, GRID_W - NA_KW)
    ok_c = (key_c[None, :] >= win_c[:, None]) & (key_c[None, :] < win_c[:, None] + NA_KW)
    dc = np.clip(key_c[None, :] - q_col[:, None], 1 - NA_KW, NA_KW - 1) + NA_KW - 1
    onehot = np.zeros((2 * NA_KW - 1, GRID_W, GRID_W), np.float32)
    onehot[dc, q_col[:, None], key_c[None, :]] = 1.0
    col_tab = jnp.einsum("hab,bck->hack", rpb.astype(F32), jnp.asarray(onehot),
                         precision=lax.Precision.HIGHEST)
    col_tab = jnp.where(jnp.asarray(ok_c)[None, None], col_tab, NEG)
    tabs = []
    for (r, win_rel) in cases:
        q_row = np.arange(qr)
        key_r = r + np.arange(kb)
        win_r = np.asarray(win_rel)
        ok_r = (key_r[None, :] >= win_r[:, None]) & (key_r[None, :] < win_r[:, None] + kh)
        dr = np.clip(key_r[None, :] - q_row[:, None], 1 - NA_KH, NA_KH - 1) + NA_KH - 1
        blk = col_tab[:, dr.reshape(-1)].reshape(nh, qr, kb, GRID_W, GRID_W)
        blk = jnp.where(jnp.asarray(ok_r)[None, :, :, None, None], blk, NEG)
        tabs.append(blk.transpose(0, 1, 3, 2, 4).reshape(nh, qr * GRID_W, kb * GRID_W))
    return jnp.stack(tabs, axis=1), np.asarray(case_of, np.int32), band_r.astype(np.int32)


def _na_kernel(case_ref, start_ref, q_ref, k_ref, v_ref, kx_ref, vx_ref, bias_ref, o_ref, *, nkeys):
    i = pl.program_id(2)
    start = pl.multiple_of(start_ref[i], GRID_W)
    q = q_ref[...]
    kb = k_ref[pl.ds(start, nkeys), :]
    vb = v_ref[pl.ds(start, nkeys), :]
    dn = (((1,), (1,)), ((), ()))
    s_loc = lax.dot_general(q, kb, dn, preferred_element_type=F32) * ATTN_SCALE + bias_ref[0, 0]
    s_ctx = lax.dot_general(q, kx_ref[...], dn, preferred_element_type=F32) * ATTN_SCALE
    m = jnp.maximum(jnp.max(s_loc, axis=-1, keepdims=True), jnp.max(s_ctx, axis=-1, keepdims=True))
    p_loc = jnp.exp(s_loc - m)
    p_ctx = jnp.exp(s_ctx - m)
    den = jnp.sum(p_loc, axis=-1, keepdims=True) + jnp.sum(p_ctx, axis=-1, keepdims=True)
    o = (jnp.dot(p_loc.astype(BF16), vb, preferred_element_type=F32)
         + jnp.dot(p_ctx.astype(BF16), vx_ref[...], preferred_element_type=F32))
    o_ref[...] = (o / den).astype(o_ref.dtype)


def neighbourhood_attention(p, rpb, seq, n_ctx, nb, heads, out_rows):
    rows = seq // GRID_W
    kh, qr, kb, n_rb, _ = _na_geometry(rows)
    bias, case_of, band_r = _na_bias(rpb, rows)
    tq = qr * GRID_W
    nkeys = kb * GRID_W
    qpb = seq // tq
    xb0 = nb * seq // n_ctx
    grid_spec = pltpu.PrefetchScalarGridSpec(
        num_scalar_prefetch=2,
        grid=(nb, heads, n_rb),
        in_specs=[
            pl.BlockSpec((tq, HEAD_DIM), lambda b, h, i, cs, st: (b * qpb + i, h)),
            pl.BlockSpec((seq, HEAD_DIM), lambda b, h, i, cs, st: (b, heads + h)),
            pl.BlockSpec((seq, HEAD_DIM), lambda b, h, i, cs, st: (b, 2 * heads + h)),
            pl.BlockSpec((n_ctx, HEAD_DIM), lambda b, h, i, cs, st: (xb0 + b, heads + h)),
            pl.BlockSpec((n_ctx, HEAD_DIM), lambda b, h, i, cs, st: (xb0 + b, 2 * heads + h)),
            pl.BlockSpec((1, 1, tq, nkeys), lambda b, h, i, cs, st: (h, cs[i], 0, 0)),
        ],
        out_specs=pl.BlockSpec((tq, HEAD_DIM), lambda b, h, i, cs, st: (b * qpb + i, h)),
    )
    return pl.pallas_call(
        functools.partial(_na_kernel, nkeys=nkeys),
        grid_spec=grid_spec,
        out_shape=jax.ShapeDtypeStruct((out_rows, heads * HEAD_DIM), BF16),
        compiler_params=_cp(("arbitrary",) * 3), name="neighbourhood_attention",
    )(jnp.asarray(case_of), jnp.asarray(band_r * GRID_W), p, p, p, p, p, bias)


def _ctx_attn_kernel(q_ref, k_ref, v_ref, prev_ref, o_ref):
    del prev_ref
    s = lax.dot_general(q_ref[...], k_ref[...], (((1,), (1,)), ((), ())),
                        preferred_element_type=F32) * ATTN_SCALE
    m = jnp.max(s, axis=-1, keepdims=True)
    pr = jnp.exp(s - m)
    den = jnp.sum(pr, axis=-1, keepdims=True)
    o = jnp.dot(pr.astype(BF16), v_ref[...], preferred_element_type=F32)
    o_ref[...] = (o / den).astype(o_ref.dtype)


def ctx_dense_attention(p, row0, n_ctx, nb, heads, prev):
    xb0 = row0 // n_ctx
    return pl.pallas_call(
        _ctx_attn_kernel,
        grid=(nb, heads),
        in_specs=[pl.BlockSpec((n_ctx, HEAD_DIM), lambda b, h: (xb0 + b, h)),
                  pl.BlockSpec((n_ctx, HEAD_DIM), lambda b, h: (xb0 + b, heads + h)),
                  pl.BlockSpec((n_ctx, HEAD_DIM), lambda b, h: (xb0 + b, 2 * heads + h)),
                  pl.BlockSpec(memory_space=pl.ANY)],
        out_specs=pl.BlockSpec((n_ctx, HEAD_DIM), lambda b, h: (xb0 + b, h)),
        out_shape=jax.ShapeDtypeStruct(prev.shape, BF16),
        input_output_aliases={3: 0},
        compiler_params=_cp(("arbitrary", "arbitrary")), name="ctx_dense_attention",
    )(p, p, p, prev)


MOE_TM = 256


TOK_BITS = 14


def _expert_kernel(te_ref, tv_ref, code_ref, h_hbm, wgu_ref, bgu_ref, wdn_ref, bdn_ref, y_hbm,
                   xbuf, ybuf, gsem, ssem, *, de, tm):
    i = pl.program_id(0)
    slot = i % 2
    other = 1 - slot
    dh = xbuf.shape[2]

    def gather_copy(tile, sl, r):
        tok = code_ref[(tile + 2) * tm + r] & ((1 << TOK_BITS) - 1)
        return pltpu.make_async_copy(h_hbm.at[pl.ds(tok, 1)], xbuf.at[sl, pl.ds(r, 1)], gsem.at[sl])

    def scatter_copy(tile, sl, r):
        dst = lax.shift_right_logical(code_ref[(tile + 2) * tm + r], TOK_BITS)
        return pltpu.make_async_copy(ybuf.at[sl, pl.ds(r, 1)], y_hbm.at[pl.ds(dst, 1)], ssem.at[sl])

    def start_all(copy_fn, tile, sl):
        def body(r, carry):
            copy_fn(tile, sl, r).start()
            return carry
        lax.fori_loop(0, tm, body, 0, unroll=8)

    def wait_all(buf, sem, sl):
        pltpu.make_async_copy(buf.at[sl], buf.at[sl], sem.at[sl]).wait()

    @pl.when(i == 0)
    def _():
        ybuf[...] = jnp.zeros_like(ybuf)
        start_all(gather_copy, 0, 0)
        start_all(scatter_copy, -2, 0)

    @pl.when(tv_ref[i] != 0)
    def _():
        wait_all(xbuf, gsem, slot)
        kc = 256
        n_kc = dh // kc
        g_per = tm // n_kc
        acc = None
        for c in range(n_kc):
            for r in range(c * g_per, (c + 1) * g_per):
                gather_copy(i + 1, other, r).start()
            lo, hi = _unpack_halves(xbuf[slot, :, c * kc:(c + 1) * kc])
            w_lo = wgu_ref[0, 0, c * kc:(c + 1) * kc, :].astype(BF16)
            w_hi = wgu_ref[0, 0, dh + c * kc:dh + (c + 1) * kc, :].astype(BF16)
            part = (jnp.dot(lo.astype(BF16), w_lo, preferred_element_type=F32)
                    + jnp.dot(hi.astype(BF16), w_hi, preferred_element_type=F32))
            acc = part if acc is None else acc + part
        gu = acc + bgu_ref[0, 0]
        gate = jnp.minimum(gu[:, :de], SWIGLU_LIMIT)
        up = jnp.clip(gu[:, de:], -SWIGLU_LIMIT, SWIGLU_LIMIT)
        act = (gate * _sigmoid(SWIGLU_ALPHA * gate) * (up + 1.0)).astype(BF16)
        wait_all(ybuf, ssem, slot)
        yc = 512
        n_yc = dh // yc
        s_per = tm // n_yc
        for c in range(n_yc):
            for r in range(c * s_per, (c + 1) * s_per):
                scatter_copy(i - 1, other, r).start()
            cl = slice(c * yc, (c + 1) * yc)
            ch = slice(dh + c * yc, dh + (c + 1) * yc)
            y_lo = (jnp.dot(act, wdn_ref[0, 0, :, cl].astype(BF16), preferred_element_type=F32)
                    + bdn_ref[0, 0, :, cl])
            y_hi = (jnp.dot(act, wdn_ref[0, 0, :, ch].astype(BF16), preferred_element_type=F32)
                    + bdn_ref[0, 0, :, ch])
            ybuf[slot, :, cl] = _pack_pair(y_lo.astype(BF16).astype(F32), y_hi.astype(BF16).astype(F32))

    @pl.when(jnp.logical_and(tv_ref[i] == 0, jnp.logical_and(i > 0, tv_ref[jnp.maximum(i - 1, 0)] != 0)))
    def _():
        wait_all(xbuf, gsem, slot)
        wait_all(ybuf, ssem, slot)
        start_all(scatter_copy, i - 1, other)
        wait_all(ybuf, ssem, other)


def expert_ffn(h_packed, code, tile_e, tile_valid, w_gu, b_gu, w_dn, b_dn, layer, n_slots):
    t, dh = h_packed.shape
    d = 2 * dh
    nl, ne, _, f2 = w_gu.shape
    de = f2 // 2
    tm = MOE_TM
    nt = tile_e.shape[0]
    assert code.shape[0] == (nt + 2) * tm and tm % (dh // 256) == 0 and dh % 512 == 0
    grid_spec = pltpu.PrefetchScalarGridSpec(
        num_scalar_prefetch=3,
        grid=(nt,),
        in_specs=[
            pl.BlockSpec(memory_space=pl.ANY),
            pl.BlockSpec((1, 1, d, f2), lambda i, te, tv, cd: (layer, te[i], 0, 0)),
            pl.BlockSpec((1, 1, 1, f2), lambda i, te, tv, cd: (layer, te[i], 0, 0)),
            pl.BlockSpec((1, 1, de, d), lambda i, te, tv, cd: (layer, te[i], 0, 0)),
            pl.BlockSpec((1, 1, 1, d), lambda i, te, tv, cd: (layer, te[i], 0, 0)),
        ],
        out_specs=pl.BlockSpec(memory_space=pl.ANY),
        scratch_shapes=[pltpu.VMEM((2, tm, dh), jnp.uint32), pltpu.VMEM((2, tm, dh), jnp.uint32),
                        pltpu.SemaphoreType.DMA((2,)), pltpu.SemaphoreType.DMA((2,))],
    )
    return pl.pallas_call(
        functools.partial(_expert_kernel, de=de, tm=tm),
        grid_spec=grid_spec,
        out_shape=jax.ShapeDtypeStruct((n_slots + tm, dh), jnp.uint32),
        compiler_params=_cp(("arbitrary",)), name="expert_ffn",
    )(tile_e, tile_valid, code, h_packed, w_gu, b_gu.reshape(nl, ne, 1, f2), w_dn, b_dn.reshape(nl, ne, 1, d))


def _combine_kernel(x_ref, y0_ref, y1_ref, y2_ref, y3_ref, wt_ref, g_ref, *rest, final_norm):
    o_ref = rest[-1]
    wt = wt_ref[...]
    half = y0_ref.shape[1]
    acc_lo = jnp.zeros((x_ref.shape[0], half), F32)
    acc_hi = jnp.zeros((x_ref.shape[0], half), F32)
    for k, y_ref in enumerate((y0_ref, y1_ref, y2_ref, y3_ref)):
        lo, hi = _unpack_halves(y_ref[...])
        acc_lo = acc_lo + wt[:, k:k + 1] * lo
        acc_hi = acc_hi + wt[:, k:k + 1] * hi
    x_lo = x_ref[:, :half] + g_ref[0, :, :half] * acc_lo
    x_hi = x_ref[:, half:] + g_ref[0, :, half:] * acc_hi
    if final_norm:
        nf_ref = rest[0]
        ms = (jnp.sum(x_lo * x_lo, axis=-1, keepdims=True)
              + jnp.sum(x_hi * x_hi, axis=-1, keepdims=True)) / (2 * half)
        inv = lax.rsqrt(ms + EPS)
        x_lo = x_lo * inv * nf_ref[:, :half]
        x_hi = x_hi * inv * nf_ref[:, half:]
    o_ref[:, :half] = x_lo
    o_ref[:, half:] = x_hi


def moe_combine(x, y4, wts, g3, seg, tm, norm_final=None):
    t, d = x.shape
    tiles = t // tm
    in_specs = [pl.BlockSpec((tm, d), lambda i: (i, 0))]
    in_specs += [pl.BlockSpec((tm, d // 2), functools.partial(lambda i, k: (k * tiles + i, 0), k=k))
                 for k in range(TOP_K)]
    in_specs += [pl.BlockSpec((tm, LANES), lambda i: (i, 0)),
                 pl.BlockSpec((1, 1, d), lambda i: (seg(i), 0, 0))]
    args = [x, y4, y4, y4, y4, wts, g3]
    if norm_final is not None:
        in_specs.append(pl.BlockSpec((1, d), lambda i: (0, 0)))
        args.append(norm_final.reshape(1, d))
    return pl.pallas_call(
        functools.partial(_combine_kernel, final_norm=norm_final is not None),
        grid=(tiles,), in_specs=in_specs,
        out_specs=pl.BlockSpec((tm, d), lambda i: (i, 0)),
        out_shape=jax.ShapeDtypeStruct((t, d), F32),
        compiler_params=_cp(("arbitrary",)), name="moe_combine",
    )(*args)


def moe_layer(x, g_ffn, sc3, sh3, g3, seg, tm, w_r, b_r, w_gu, b_gu, w_dn, b_dn, layer, norm_final=None):
    t, d = x.shape
    h, topi, wts, rk, cnt = modulate(x, g_ffn, sc3, sh3, seg, tm, None, w_r, b_r)
    tme = MOE_TM
    ns = t * TOP_K
    nt = -(-(ns + N_EXPERTS * (tme - 1)) // tme) + 1
    counts = cnt[0, :N_EXPERTS]
    padded = ((counts + tme - 1) // tme) * tme
    e_i = jnp.arange(N_EXPERTS, dtype=jnp.int32)
    ends = jnp.sum(jnp.where(e_i[None, :] <= e_i[:, None], padded[None, :], 0), axis=1)
    offs = ends - padded
    ef = topi[:, :TOP_K]
    onehot = ef[:, :, None] == jnp.arange(N_EXPERTS, dtype=jnp.int32)[None, None, :]
    pos = rk[:, :TOP_K] + jnp.sum(jnp.where(onehot, offs[None, None, :], 0), axis=-1)
    assert t <= (1 << TOK_BITS) and ns + tme < (1 << (31 - TOK_BITS))
    tok = jnp.arange(t, dtype=jnp.int32)[:, None]
    slot_code = tok | ((jnp.arange(TOP_K, dtype=jnp.int32)[None, :] * t + tok) << TOK_BITS)
    pad_code = (ns + (jnp.arange((nt + 2) * tme, dtype=jnp.int32) % tme)) << TOK_BITS
    code = pad_code.at[pos.reshape(ns) + 2 * tme].set(slot_code.reshape(ns))
    tile_start = jnp.arange(nt, dtype=jnp.int32) * tme
    tile_valid = (tile_start < ends[-1]).astype(jnp.int32)
    tile_e = jnp.sum((ends[None, :] <= tile_start[:, None]).astype(jnp.int32), axis=1)
    tile_e = jnp.minimum(tile_e, N_EXPERTS - 1)
    tile_e = jnp.where(tile_valid != 0, tile_e, jnp.max(jnp.where(tile_valid != 0, tile_e, 0)))
    y4 = expert_ffn(h, code, tile_e, tile_valid, w_gu, b_gu, w_dn, b_dn, layer, ns)
    return moe_combine(x, y4, wts, g3, seg, tm, norm_final)


def kernel(x, c, ctx, c_ctx, ada_w1, ada_w2, ada_b, norm_mix, norm_ffn, ev_w_in, ev_v_norm, ev_w_s, ev_b_s, ev_w_out, od_w_in, od_rpb, od_lam, od_sub_norm, od_w_out, moe_w_r, moe_b_r, moe_w_gu, moe_b_gu, moe_w_dn, moe_b_dn, norm_final):
    nb, seq, d = x.shape
    n_ctx = ctx.shape[1]
    depth = ada_w1.shape[0]
    n_lat = nb * seq
    t_all = n_lat + nb * n_ctx
    tm = 512 if (t_all % 512 == 0 and seq % 512 == 0) else 256
    tms = 256
    assert n_lat % tm == 0 and (nb * n_ctx) % tms == 0 and seq % tms == 0 and nb + 1 <= 8

    def seg_for(tile):
        return _seg_fn(n_lat // tile, seq // tile, nb)

    ev_w_in_b = cast_bf16(ev_w_in, 256)
    ev_w_out_b = cast_bf16(ev_w_out, 512)
    od_w_in_b = cast_bf16(od_w_in, 128)
    od_w_out_b = cast_bf16(od_w_out, 512)

    xs = jnp.concatenate([x.reshape(n_lat, d), ctx.reshape(nb * n_ctx, d)], axis=0)

    cv = jnp.zeros((8, d), F32).at[:nb].set(c).at[nb].set(c_ctx)
    mods = ada_all(cv, ada_w1, ada_w2, ada_b)

    for i in range(depth):
        last = i == depth - 1
        j = i // 2
        m6 = mods[i].reshape(8, 6, d)
        sh1, sc1, g1, sh2, sc2, g2 = [m6[:, q, :].reshape(8, 1, d) for q in range(6)]
        h = modulate(xs, norm_mix[i], sc1, sh1, seg_for(tms), tms)
        n_out = n_lat if last else t_all
        if i % 2 == 0:
            p = matmul(h, ev_w_in_b, j, tm)
            y1 = even_a(p, ev_v_norm[j], ev_w_s[j], ev_b_s[j], tms)
            y2 = even_b(p, seq, n_ctx, nb, tms)
            w_out = ev_w_out_b
        else:
            lambda_init = 0.8 - 0.6 * math.exp(-0.3 * i)
            p = matmul(h, od_w_in_b, j, tm)
            cw = p.shape[1] // 6
            heads_c = cw // HEAD_DIM
            heads_d = cw // (2 * HEAD_DIM)
            y1 = neighbourhood_attention(p, od_rpb[j], seq, n_ctx, nb, heads_c, n_out)
            qk = rope_qk(p, seq, nb, tms)
            nblk = cw // (2 * HEAD_DIM)
            seg_x = (p, n_lat, 4 * nblk, p, n_lat, 5 * nblk, n_ctx)
            seg_l = (qk, 0, nblk, p, 0, 5 * nblk, seq)
            y2 = diff_attention(qk, 0, 0, seq, [seg_x, seg_l], od_lam[j], od_sub_norm[j], nb, heads_d,
                                lambda_init, 256, n_out, 0)
            if not last:
                y1 = ctx_dense_attention(p, n_lat, n_ctx, nb, heads_c, y1)
                y2 = diff_attention(p, n_lat, 3 * nblk, n_ctx, [seg_x], od_lam[j], od_sub_norm[j], nb,
                                    heads_d, lambda_init, 256, n_out, n_lat, prev=y2)
            w_out = od_w_out_b
        xs = matmul_residual(y1, y2, w_out, j, xs, g1, seg_for(tm), tm, n_out)
        xs = moe_layer(xs, norm_ffn[i], sc2, sh2, g2, seg_for(tms), tms, moe_w_r[i], moe_b_r[i],
                       moe_w_gu, moe_b_gu, moe_w_dn, moe_b_dn, i, norm_final if last else None)
    return xs.reshape(nb, seq, d)
```

```python
import functools
import math

import numpy as np
import jax
import jax.numpy as jnp
from jax import lax
from jax.experimental import pallas as pl
from jax.experimental.pallas import tpu as pltpu

F32 = jnp.float32
BF16 = jnp.bfloat16

GRID_W = 64
HEAD_DIM = 128
EPS = 1e-6
ROPE_THETA = 10000.0
CHUNK = 128
NA_KH = 8
NA_KW = 16
NA_QR = 8
ATTN_SCALE = HEAD_DIM ** -0.5
NEG = -1e30
N_EXPERTS = 32
TOP_K = 4
SWIGLU_LIMIT = 7.0
SWIGLU_ALPHA = 1.702
LANES = 128

VMEM_LIMIT = 56 * 1024 * 1024


def _cp(sem, vmem=VMEM_LIMIT):
    return pltpu.CompilerParams(dimension_semantics=sem, vmem_limit_bytes=vmem)


def _gelu_tanh(x):
    return 0.5 * x * (1.0 + jnp.tanh(0.7978845608028654 * (x + 0.044715 * (x * x * x))))


def _sigmoid(x):
    return 1.0 / (1.0 + jnp.exp(-x))


def _pack_halves(x):
    c = x.shape[1] // 2
    return _pack_pair(x[:, :c], x[:, c:])


def _pack_pair(left, right):
    lo = lax.shift_right_logical(lax.bitcast_convert_type(left, jnp.uint32), jnp.uint32(16))
    hi = lax.bitcast_convert_type(right, jnp.uint32) & jnp.uint32(0xFFFF0000)
    return lo | hi


def _unpack_halves(w):
    lo = lax.bitcast_convert_type(lax.shift_left(w, jnp.uint32(16)), F32)
    hi = lax.bitcast_convert_type(w & jnp.uint32(0xFFFF0000), F32)
    return lo, hi


def _pick(n, prefs):
    for p in prefs:
        if n % p == 0:
            return p
    raise ValueError(f"no tile for {n}")


def _ada_kernel(cv_ref, w1_ref, w2_ref, b_ref, o_ref):
    cv = cv_ref[...]
    s = cv * _sigmoid(cv)
    t = jnp.dot(s, w1_ref[0], preferred_element_type=F32, precision=lax.Precision.HIGHEST)
    m = jnp.dot(t, w2_ref[0], preferred_element_type=F32, precision=lax.Precision.HIGHEST)
    o_ref[0] = m + b_ref[0]


def ada_all(cv, w1, w2, b):
    nl, d, r = w1.shape
    n6 = w2.shape[2]
    tn = 2048
    return pl.pallas_call(
        _ada_kernel,
        grid=(nl, n6 // tn),
        in_specs=[
            pl.BlockSpec((8, d), lambda l, j: (0, 0)),
            pl.BlockSpec((1, d, r), lambda l, j: (l, 0, 0)),
            pl.BlockSpec((1, r, tn), lambda l, j: (l, 0, j)),
            pl.BlockSpec((1, 1, tn), lambda l, j: (l, 0, j)),
        ],
        out_specs=pl.BlockSpec((1, 8, tn), lambda l, j: (l, 0, j)),
        out_shape=jax.ShapeDtypeStruct((nl, 8, n6), F32),
        compiler_params=_cp(("arbitrary", "arbitrary")),
        name="ada_mod",
    )(cv, w1, w2, b.reshape(nl, 1, n6))


def _seg_fn(n_lat_tiles, tiles_per_batch, n_batch):
    def seg(i):
        return jnp.where(i < n_lat_tiles, i // tiles_per_batch, n_batch)
    return seg


def _mod_kernel(x_ref, g_ref, sc_ref, sh_ref, o_ref):
    x = x_ref[...]
    ms = jnp.mean(x * x, axis=-1, keepdims=True)
    y = x * lax.rsqrt(ms + EPS) * g_ref[...]
    o_ref[...] = (y * (1.0 + sc_ref[0]) + sh_ref[0]).astype(o_ref.dtype)


def _mod_router_kernel(x_ref, g_ref, sc_ref, sh_ref, wr_ref, br_ref, o_ref, ti_ref, wt_ref, rk_ref, cnt_ref,
                       base_ref):
    @pl.when(pl.program_id(0) == 0)
    def _():
        base_ref[...] = jnp.zeros_like(base_ref)

    x = x_ref[...]
    ms = jnp.mean(x * x, axis=-1, keepdims=True)
    y = x * lax.rsqrt(ms + EPS) * g_ref[...]
    h = y * (1.0 + sc_ref[0]) + sh_ref[0]
    h_hi = h.astype(BF16)
    o_ref[...] = _pack_halves(h_hi.astype(F32))
    h_lo = (h - h_hi.astype(F32)).astype(BF16)
    w = wr_ref[...]
    w_hi = w.astype(BF16)
    w_lo = (w - w_hi.astype(F32)).astype(BF16)
    logits = (jnp.dot(h_hi, w_hi, preferred_element_type=F32)
              + jnp.dot(h_hi, w_lo, preferred_element_type=F32)
              + jnp.dot(h_lo, w_hi, preferred_element_type=F32)) + br_ref[...]
    tm = logits.shape[0]
    lane = lax.broadcasted_iota(jnp.int32, logits.shape, 1).astype(F32)
    vals, idxs = [], []
    cur = logits
    for _ in range(TOP_K):
        m = jnp.max(cur, axis=-1, keepdims=True)
        idx = jnp.min(jnp.where(cur == m, lane, float(LANES)), axis=-1, keepdims=True)
        vals.append(m)
        idxs.append(idx)
        cur = jnp.where(lane == idx, -jnp.inf, cur)
    es = [jnp.exp(v - vals[0]) for v in vals]
    den = es[0] + es[1] + es[2] + es[3]
    sel = [lane == idxs[k] for k in range(TOP_K)]
    osum = jnp.zeros(logits.shape, F32)
    for k in range(TOP_K):
        osum = osum + jnp.where(sel[k], 1.0, 0.0)
    r_i = lax.broadcasted_iota(jnp.int32, (tm, tm), 0)
    c_i = lax.broadcasted_iota(jnp.int32, (tm, tm), 1)
    lower = jnp.where(c_i < r_i, 1.0, 0.0).astype(BF16)
    pref = jnp.dot(lower, osum.astype(BF16), preferred_element_type=F32) + base_ref[...]
    ti = jnp.zeros(logits.shape, F32)
    wt = jnp.zeros(logits.shape, F32)
    rk = jnp.zeros(logits.shape, F32)
    for k in range(TOP_K):
        rank_k = jnp.sum(jnp.where(sel[k], pref, 0.0), axis=-1, keepdims=True)
        ti = jnp.where(lane == float(k), idxs[k], ti)
        wt = jnp.where(lane == float(k), es[k] / den, wt)
        rk = jnp.where(lane == float(k), rank_k, rk)
    ti_ref[...] = ti.astype(jnp.int32)
    wt_ref[...] = wt
    rk_ref[...] = rk.astype(jnp.int32)
    base_ref[...] += jnp.sum(osum, axis=0, keepdims=True)
    cnt_ref[...] = base_ref[...].astype(jnp.int32)


def modulate(x, g, sc3, sh3, seg, tm, m_rows=None, w_r=None, b_r=None):
    t, d = x.shape
    t = t if m_rows is None else m_rows
    in_specs = [
        pl.BlockSpec((tm, d), lambda i: (i, 0)),
        pl.BlockSpec((1, d), lambda i: (0, 0)),
        pl.BlockSpec((1, 1, d), lambda i: (seg(i), 0, 0)),
        pl.BlockSpec((1, 1, d), lambda i: (seg(i), 0, 0)),
    ]
    if w_r is None:
        return pl.pallas_call(
            _mod_kernel, grid=(t // tm,), in_specs=in_specs,
            out_specs=pl.BlockSpec((tm, d), lambda i: (i, 0)),
            out_shape=jax.ShapeDtypeStruct((t, d), BF16),
            compiler_params=_cp(("arbitrary",)), name="modulate",
        )(x, g.reshape(1, d), sc3, sh3)
    wr_pad = jnp.zeros((d, LANES), F32).at[:, :N_EXPERTS].set(w_r)
    br_pad = jnp.full((1, LANES), NEG, F32).at[0, :N_EXPERTS].set(b_r)
    in_specs += [pl.BlockSpec((d, LANES), lambda i: (0, 0)), pl.BlockSpec((1, LANES), lambda i: (0, 0))]
    row_spec = pl.BlockSpec((tm, LANES), lambda i: (i, 0))
    return pl.pallas_call(
        _mod_router_kernel, grid=(t // tm,), in_specs=in_specs,
        out_specs=[pl.BlockSpec((tm, d // 2), lambda i: (i, 0)), row_spec, row_spec, row_spec,
                   pl.BlockSpec((1, LANES), lambda i: (0, 0))],
        out_shape=[jax.ShapeDtypeStruct((t, d // 2), jnp.uint32),
                   jax.ShapeDtypeStruct((t, LANES), jnp.int32),
                   jax.ShapeDtypeStruct((t, LANES), F32),
                   jax.ShapeDtypeStruct((t, LANES), jnp.int32),
                   jax.ShapeDtypeStruct((1, LANES), jnp.int32)],
        scratch_shapes=[pltpu.VMEM((1, LANES), F32)],
        compiler_params=_cp(("arbitrary",)), name="modulate_router",
    )(x, g.reshape(1, d), sc3, sh3, wr_pad, br_pad)


def _cast_kernel(x_ref, o_ref):
    o_ref[...] = x_ref[...].astype(o_ref.dtype)


def cast_bf16(w, block_rows):
    shape = w.shape
    w2 = w.reshape(-1, shape[-1])
    r, c = w2.shape
    out = pl.pallas_call(
        _cast_kernel, grid=(r // block_rows,),
        in_specs=[pl.BlockSpec((block_rows, c), lambda i: (i, 0))],
        out_specs=pl.BlockSpec((block_rows, c), lambda i: (i, 0)),
        out_shape=jax.ShapeDtypeStruct((r, c), BF16),
        compiler_params=_cp(("arbitrary",)), name="cast_bf16",
    )(w2)
    return out.reshape(shape)


def _mm_kernel(a_ref, b_ref, o_ref):
    o_ref[...] = jnp.dot(a_ref[...], b_ref[0], preferred_element_type=F32).astype(o_ref.dtype)


def matmul(a, b3, layer, tm, out_dtype=BF16, tn=1024):
    m, kk = a.shape
    n = b3.shape[2]
    tn = _pick(n, (tn, 512, 256))
    return pl.pallas_call(
        _mm_kernel,
        grid=(n // tn, m // tm),
        in_specs=[pl.BlockSpec((tm, kk), lambda j, i: (i, 0)),
                  pl.BlockSpec((1, kk, tn), lambda j, i: (layer, 0, j))],
        out_specs=pl.BlockSpec((tm, tn), lambda j, i: (i, j)),
        out_shape=jax.ShapeDtypeStruct((m, n), out_dtype),
        compiler_params=_cp(("arbitrary", "arbitrary")), name="matmul",
    )(a, b3)


def _mm_res_kernel(a1_ref, a2_ref, b_ref, x_ref, g_ref, o_ref):
    k1 = a1_ref.shape[1]
    acc = jnp.dot(a1_ref[...], b_ref[0, :k1, :], preferred_element_type=F32)
    acc = acc + jnp.dot(a2_ref[...], b_ref[0, k1:, :], preferred_element_type=F32)
    o_ref[...] = x_ref[...] + g_ref[0] * acc


def matmul_residual(a1, a2, b3, layer, x, g3, seg, tm, m_rows, tn=1024):
    k1, k2 = a1.shape[1], a2.shape[1]
    n = b3.shape[2]
    tn = _pick(n, (tn, 512, 256))
    return pl.pallas_call(
        _mm_res_kernel,
        grid=(n // tn, m_rows // tm),
        in_specs=[pl.BlockSpec((tm, k1), lambda j, i: (i, 0)),
                  pl.BlockSpec((tm, k2), lambda j, i: (i, 0)),
                  pl.BlockSpec((1, k1 + k2, tn), lambda j, i: (layer, 0, j)),
                  pl.BlockSpec((tm, tn), lambda j, i: (i, j)),
                  pl.BlockSpec((1, 1, tn), lambda j, i: (seg(i), 0, j))],
        out_specs=pl.BlockSpec((tm, tn), lambda j, i: (i, j)),
        out_shape=jax.ShapeDtypeStruct((m_rows, n), F32),
        compiler_params=_cp(("arbitrary", "arbitrary")), name="matmul_residual",
    )(a1, a2, b3, x, g3)


def _bmm_kernel(a_ref, b1_ref, b2_ref, *rest, nkh):
    o_ref, acc_ref = rest[-2], rest[-1]
    k = pl.program_id(3)

    @pl.when(k == 0)
    def _():
        acc_ref[...] = jnp.zeros_like(acc_ref)

    @pl.when(k < nkh)
    def _():
        acc_ref[...] += jnp.dot(a_ref[...], b1_ref[0], preferred_element_type=F32)

    @pl.when(k >= nkh)
    def _():
        acc_ref[...] += jnp.dot(a_ref[...], b2_ref[0], preferred_element_type=F32)

    @pl.when(k == 2 * nkh - 1)
    def _():
        o_ref[...] = acc_ref[...].astype(o_ref.dtype)


def shared_lhs_bmm(a, b1, b2, out_rows, row0, prev=None):
    m, kk2 = a.shape
    nb, kk, n = b1.shape
    tm, tn, tk = _pick(m, (512, 256, 128)), _pick(n, (1024, 512)), _pick(kk, (4096, 2048, 1024, 512, 256, 128))
    nkh = kk // tk
    nk = 2 * nkh
    r0 = row0 // tm
    mt = m // tm
    in_specs = [pl.BlockSpec((tm, tk), lambda z, j, i, k: (i, k)),
                pl.BlockSpec((1, tk, tn), lambda z, j, i, k: (z, jnp.minimum(k, nkh - 1), j)),
                pl.BlockSpec((1, tk, tn), lambda z, j, i, k: (z, jnp.maximum(k - nkh, 0), j))]
    args = [a, b1, b2]
    aliases = {}
    if prev is not None:
        in_specs.append(pl.BlockSpec(memory_space=pl.ANY))
        args.append(prev)
        aliases = {3: 0}
    return pl.pallas_call(
        functools.partial(_bmm_kernel, nkh=nkh),
        grid=(nb, n // tn, mt, nk),
        in_specs=in_specs,
        out_specs=pl.BlockSpec((tm, tn), lambda z, j, i, k: (r0 + z * mt + i, j)),
        out_shape=jax.ShapeDtypeStruct((out_rows, n), BF16),
        scratch_shapes=[pltpu.VMEM((tm, tn), F32)],
        input_output_aliases=aliases,
        compiler_params=_cp(("arbitrary",) * 4), name="dft_seq_matmul",
    )(*args)


def _even_a_kernel(u_ref, v_ref, gv_ref, ws_ref, bs_ref, o_ref, *, tm, heads):
    for h in range(heads):
        cs = slice(h * HEAD_DIM, (h + 1) * HEAD_DIM)
        u = _gelu_tanh(u_ref[:, cs].astype(F32))
        v = _gelu_tanh(v_ref[:, cs].astype(F32))
        v = v * lax.rsqrt(jnp.mean(v * v, axis=-1, keepdims=True) + EPS) * gv_ref[:, cs]
        vb = v.astype(BF16)
        w = ws_ref[h]
        bias = bs_ref[h]
        for c in range(tm // CHUNK):
            rs = slice(c * CHUNK, (c + 1) * CHUNK)
            mixed = jnp.dot(w, vb[rs], preferred_element_type=F32) + bias
            o_ref[rs, cs] = (u[rs] * mixed).astype(o_ref.dtype)


def even_a(p, g_v, w_s, b_s, tm):
    t = p.shape[0]
    heads = w_s.shape[0]
    aw = heads * HEAD_DIM
    bsb = jnp.broadcast_to(b_s[:, :, None], (heads, CHUNK, HEAD_DIM)).astype(F32)
    return pl.pallas_call(
        functools.partial(_even_a_kernel, tm=tm, heads=heads),
        grid=(t // tm,),
        in_specs=[pl.BlockSpec((tm, aw), lambda i: (i, 0)),
                  pl.BlockSpec((tm, aw), lambda i: (i, 1)),
                  pl.BlockSpec((1, aw), lambda i: (0, 0)),
                  pl.BlockSpec((heads, CHUNK, CHUNK), lambda i: (0, 0, 0)),
                  pl.BlockSpec((heads, CHUNK, HEAD_DIM), lambda i: (0, 0, 0))],
        out_specs=pl.BlockSpec((tm, aw), lambda i: (i, 0)),
        out_shape=jax.ShapeDtypeStruct((t, aw), BF16),
        compiler_params=_cp(("arbitrary",)), name="even_chunk_gmlp",
    )(p, p, g_v.reshape(1, aw), w_s.astype(BF16), bsb)


def _even_b1_kernel(z_ref, cs_ref, oc_ref, os_ref, *, groups):
    for g in range(groups):
        cs = slice(g * HEAD_DIM, (g + 1) * HEAD_DIM)
        r = jnp.dot(z_ref[:, cs], cs_ref[...], preferred_element_type=F32)
        oc_ref[0, :, cs] = r[:, :HEAD_DIM].astype(oc_ref.dtype)
        os_ref[0, :, cs] = r[:, HEAD_DIM:].astype(os_ref.dtype)


def even_b_feature_dft(p, cs_mat, row0, n, nb, tm):
    bw = p.shape[1] // 3
    groups = bw // HEAD_DIM
    tpb = n // tm
    t0 = row0 // tm
    out_spec = pl.BlockSpec((1, tm, bw), lambda b, j: (b, j, 0))
    return pl.pallas_call(
        functools.partial(_even_b1_kernel, groups=groups),
        grid=(nb, tpb),
        in_specs=[pl.BlockSpec((tm, bw), lambda b, j: (t0 + b * tpb + j, 2)),
                  pl.BlockSpec((HEAD_DIM, 2 * HEAD_DIM), lambda b, j: (0, 0))],
        out_specs=[out_spec, out_spec],
        out_shape=[jax.ShapeDtypeStruct((nb, n, bw), BF16), jax.ShapeDtypeStruct((nb, n, bw), BF16)],
        compiler_params=_cp(("arbitrary", "arbitrary")), name="dft_feature",
    )(p, cs_mat)


def _dft_mats(n):
    k = jnp.arange(n, dtype=jnp.int32)
    s = 1.0 / math.sqrt(n)
    b = 64
    if n % b or n <= b:
        ang = ((k[:, None] * k[None, :]) % n).astype(F32) * (2.0 * math.pi / n)
        return jnp.cos(ang) * s, jnp.sin(ang) * s
    a = n // b
    ang_a = ((k[:, None] * (jnp.arange(a, dtype=jnp.int32) * b)[None, :]) % n).astype(F32) * (2.0 * math.pi / n)
    ang_b = ((k[:, None] * jnp.arange(b, dtype=jnp.int32)[None, :]) % n).astype(F32) * (2.0 * math.pi / n)
    ca, sa = jnp.cos(ang_a)[:, :, None] * s, jnp.sin(ang_a)[:, :, None] * s
    cb, sb = jnp.cos(ang_b)[:, None, :], jnp.sin(ang_b)[:, None, :]
    return (ca * cb - sa * sb).reshape(n, n), (sa * cb + ca * sb).reshape(n, n)


def even_b(p, n_lat, n_ctx, nb, tm):
    c_d, s_d = _dft_mats(HEAD_DIM)
    cs_mat = jnp.concatenate([c_d, s_d], axis=1).astype(BF16)
    t_all = p.shape[0]
    y = None
    for row0, n in ((0, n_lat), (nb * n_lat, n_ctx)):
        zc, zs = even_b_feature_dft(p, cs_mat, row0, n, nb, min(tm, n))
        c_n, s_n = _dft_mats(n)
        fcat = jnp.concatenate([c_n, -s_n], axis=1).astype(BF16)
        y = shared_lhs_bmm(fcat, zc, zs, t_all, row0, prev=y)
    return y


def _rope_kernel(q_ref, k_ref, cos_ref, sin_ref, o_ref, *, nblk):
    c = cos_ref[...]
    s = sin_ref[...]
    for src, base in ((q_ref, 0), (k_ref, nblk)):
        for j in range(nblk):
            t = src[:, j * HEAD_DIM:(j + 1) * HEAD_DIM].astype(F32)
            r = t * c + pltpu.roll(t, HEAD_DIM // 2, 1) * s
            o_ref[:, (base + j) * HEAD_DIM:(base + j + 1) * HEAD_DIM] = r.astype(o_ref.dtype)


def rope_qk(p, seq, nb, tm):
    dw = p.shape[1] // 6
    pos = jnp.arange(seq)
    row = (pos // GRID_W).astype(F32)
    col = (pos % GRID_W).astype(F32)
    nf = HEAD_DIM // 4
    inv = ROPE_THETA ** (-jnp.arange(nf, dtype=F32) / nf)
    ang = jnp.concatenate([row[:, None] * inv, col[:, None] * inv], axis=-1)
    cos, sin = jnp.cos(ang), jnp.sin(ang)
    cos2 = jnp.concatenate([cos, cos], axis=-1)
    sin2 = jnp.concatenate([-sin, sin], axis=-1)
    tpb = seq // tm
    return pl.pallas_call(
        functools.partial(_rope_kernel, nblk=dw // HEAD_DIM),
        grid=(nb * tpb,),
        in_specs=[pl.BlockSpec((tm, dw), lambda i: (i, 3)),
                  pl.BlockSpec((tm, dw), lambda i: (i, 4)),
                  pl.BlockSpec((tm, HEAD_DIM), lambda i: (i % tpb, 0)),
                  pl.BlockSpec((tm, HEAD_DIM), lambda i: (i % tpb, 0))],
        out_specs=pl.BlockSpec((tm, 2 * dw), lambda i: (i, 0)),
        out_shape=jax.ShapeDtypeStruct((nb * seq, 2 * dw), BF16),
        compiler_params=_cp(("arbitrary",)), name="rope",
    )(p, p, cos2, sin2)


def _diff_attn_kernel(*refs, seg_lens, kc, lambda_init):
    nseg = len(seg_lens)
    q_ref = refs[0]
    lam_ref, g_ref = refs[1 + 2 * nseg], refs[2 + 2 * nseg]
    o_ref = refs[-1]
    lp = lam_ref[...]
    lam = (jnp.exp(jnp.sum(lp[0:1] * lp[1:2], axis=-1, keepdims=True))
           - jnp.exp(jnp.sum(lp[2:3] * lp[3:4], axis=-1, keepdims=True)) + lambda_init)
    tq = q_ref.shape[0]
    dv = o_ref.shape[1]
    qs = (q_ref[:, :HEAD_DIM], q_ref[:, HEAD_DIM:])
    chunks = []
    for si in range(nseg):
        n = seg_lens[si]
        step = min(kc, n)
        chunks += [(refs[1 + 2 * si], refs[2 + 2 * si], c0, step) for c0 in range(0, n, step)]

    def scores(t, k_ref, c0, step):
        return lax.dot_general(qs[t], k_ref[c0:c0 + step, t * HEAD_DIM:(t + 1) * HEAD_DIM],
                               (((1,), (1,)), ((), ())), preferred_element_type=F32)

    def fold(x, op):
        r = x[:, :LANES]
        for j in range(1, x.shape[1] // LANES):
            r = op(r, x[:, j * LANES:(j + 1) * LANES])
        return r

    mx = [None, None]
    for (k_ref, _, c0, step) in chunks:
        for t in range(2):
            f = fold(scores(t, k_ref, c0, step), jnp.maximum)
            mx[t] = f if mx[t] is None else jnp.maximum(mx[t], f)
    c_exp = ATTN_SCALE * math.log2(math.e)
    mc = [jnp.max(mx[t], axis=-1, keepdims=True) * c_exp for t in range(2)]
    lsum = [None, None]
    acc = [None, None]
    for (k_ref, v_ref, c0, step) in chunks:
        vblk = v_ref[c0:c0 + step, :]
        for t in range(2):
            pr = jnp.exp2(scores(t, k_ref, c0, step) * c_exp - mc[t])
            f = fold(pr, jnp.add)
            lsum[t] = f if lsum[t] is None else lsum[t] + f
            pv = jnp.dot(pr.astype(BF16), vblk, preferred_element_type=F32)
            acc[t] = pv if acc[t] is None else acc[t] + pv
    l = [jnp.sum(lsum[t], axis=-1, keepdims=True) for t in range(2)]
    o = acc[0] / l[0] - lam * (acc[1] / l[1])
    o = o * lax.rsqrt(jnp.mean(o * o, axis=-1, keepdims=True) + EPS) * g_ref[...]
    o_ref[...] = (o * (1.0 - lambda_init)).astype(o_ref.dtype)


def diff_attention(q_arr, q_row0, q_col0, nq, segs, lam_p, g_sub, nb, heads, lambda_init, tq,
                   out_rows, out_row0, prev=None):
    dv = 2 * HEAD_DIM
    tq = min(tq, nq)
    qpb = nq // tq
    in_specs = [pl.BlockSpec((tq, dv), lambda b, h, i: (q_row0 // tq + b * qpb + i, q_col0 + h))]
    args = [q_arr]
    seg_lens = []
    for (k_arr, k_row0, k_col0, v_arr, v_row0, v_col0, n) in segs:
        in_specs.append(pl.BlockSpec((n, dv), functools.partial(
            lambda b, h, i, r0, c0, nn: (r0 // nn + b, c0 + h), r0=k_row0, c0=k_col0, nn=n)))
        in_specs.append(pl.BlockSpec((n, dv), functools.partial(
            lambda b, h, i, r0, c0, nn: (r0 // nn + b, c0 + h), r0=v_row0, c0=v_col0, nn=n)))
        args += [k_arr, v_arr]
        seg_lens.append(n)
    in_specs += [pl.BlockSpec((4, HEAD_DIM), lambda b, h, i: (0, 0)),
                 pl.BlockSpec((1, dv), lambda b, h, i: (0, 0))]
    args += [lam_p.astype(F32), g_sub.reshape(1, dv)]
    aliases = {}
    if prev is not None:
        in_specs.append(pl.BlockSpec(memory_space=pl.ANY))
        aliases = {len(args): 0}
        args.append(prev)
    o0 = out_row0 // tq
    return pl.pallas_call(
        functools.partial(_diff_attn_kernel, seg_lens=tuple(seg_lens), kc=1024, lambda_init=lambda_init),
        grid=(nb, heads, qpb),
        in_specs=in_specs,
        out_specs=pl.BlockSpec((tq, dv), lambda b, h, i: (o0 + b * qpb + i, h)),
        out_shape=jax.ShapeDtypeStruct((out_rows, heads * dv), BF16),
        input_output_aliases=aliases,
        compiler_params=_cp(("arbitrary",) * 3), name="diff_attention",
    )(*args)


def _na_geometry(rows):
    kh = min(NA_KH, rows)
    qr = math.gcd(rows, NA_QR)
    kb = min(kh + qr - 1, rows)
    n_rb = rows // qr
    band_r = np.clip(np.arange(n_rb) * qr - kh // 2, 0, rows - kb)
    return kh, qr, kb, n_rb, band_r


def _na_bias(rpb, rows):
    kh, qr, kb, n_rb, band_r = _na_geometry(rows)
    nh = rpb.shape[0]
    rel = band_r - np.arange(n_rb) * qr
    cases, case_of = [], []
    for i in range(n_rb):
        q_row = i * qr + np.arange(qr)
        win_r = np.clip(q_row - kh // 2, 0, rows - kh)
        key = (int(rel[i]), tuple((win_r - i * qr).tolist()))
        if key not in cases:
            cases.append(key)
        case_of.append(cases.index(key))
    q_col = np.arange(GRID_W)
    key_c = np.arange(GRID_W)
    win_c = np.clip(q_col - NA_KW // 2, 0, GRID_W - NA_KW)
    ok_c = (key_c[None, :] >= win_c[:, None]) & (key_c[None, :] < win_c[:, None] + NA_KW)
    dc = np.clip(key_c[None, :] - q_col[:, None], 1 - NA_KW, NA_KW - 1) + NA_KW - 1
    onehot = np.zeros((2 * NA_KW - 1, GRID_W, GRID_W), np.float32)
    onehot[dc, q_col[:, None], key_c[None, :]] = 1.0
    col_tab = jnp.einsum("hab,bck->hack", rpb.astype(F32), jnp.asarray(onehot),
                         precision=lax.Precision.HIGHEST)
    col_tab = jnp.where(jnp.asarray(ok_c)[None, None], col_tab, NEG)
    tabs = []
    for (r, win_rel) in cases:
        q_row = np.arange(qr)
        key_r = r + np.arange(kb)
        win_r = np.asarray(win_rel)
        ok_r = (key_r[None, :] >= win_r[:, None]) & (key_r[None, :] < win_r[:, None] + kh)
        dr = np.clip(key_r[None, :] - q_row[:, None], 1 - NA_KH, NA_KH - 1) + NA_KH - 1
        blk = col_tab[:, dr.reshape(-1)].reshape(nh, qr, kb, GRID_W, GRID_W)
        blk = jnp.where(jnp.asarray(ok_r)[None, :, :, None, None], blk, NEG)
        tabs.append(blk.transpose(0, 1, 3, 2, 4).reshape(nh, qr * GRID_W, kb * GRID_W))
    return jnp.stack(tabs, axis=1), np.asarray(case_of, np.int32), band_r.astype(np.int32)


NA_BLOCKS_PER_STEP = 2


def _na_kernel(case_ref, start_ref, q_ref, k_ref, v_ref, kx_ref, vx_ref, *rest, nkeys, tq):
    bias_refs, o_ref = rest[:-1], rest[-1]
    i = pl.program_id(2)
    dn = (((1,), (1,)), ((), ()))
    for sub, bias_ref in enumerate(bias_refs):
        start = pl.multiple_of(start_ref[i * len(bias_refs) + sub], GRID_W)
        rs = slice(sub * tq, (sub + 1) * tq)
        q = q_ref[rs, :]
        kb = k_ref[pl.ds(start, nkeys), :]
        vb = v_ref[pl.ds(start, nkeys), :]
        s_loc = lax.dot_general(q, kb, dn, preferred_element_type=F32) * ATTN_SCALE + bias_ref[0, 0]
        s_ctx = lax.dot_general(q, kx_ref[...], dn, preferred_element_type=F32) * ATTN_SCALE
        m = jnp.maximum(jnp.max(s_loc, axis=-1, keepdims=True), jnp.max(s_ctx, axis=-1, keepdims=True))
        p_loc = jnp.exp(s_loc - m)
        p_ctx = jnp.exp(s_ctx - m)
        den = jnp.sum(p_loc, axis=-1, keepdims=True) + jnp.sum(p_ctx, axis=-1, keepdims=True)
        o = (jnp.dot(p_loc.astype(BF16), vb, preferred_element_type=F32)
             + jnp.dot(p_ctx.astype(BF16), vx_ref[...], preferred_element_type=F32))
        o_ref[rs, :] = (o / den).astype(o_ref.dtype)


def neighbourhood_attention(p, rpb, seq, n_ctx, nb, heads, out_rows):
    rows = seq // GRID_W
    kh, qr, kb, n_rb, _ = _na_geometry(rows)
    bias, case_of, band_r = _na_bias(rpb, rows)
    tq = qr * GRID_W
    nkeys = kb * GRID_W
    nsub = NA_BLOCKS_PER_STEP if n_rb % NA_BLOCKS_PER_STEP == 0 else 1
    qpb = seq // (nsub * tq)
    xb0 = nb * seq // n_ctx
    bias_specs = [pl.BlockSpec((1, 1, tq, nkeys), functools.partial(
        lambda b, h, i, cs, st, sub: (h, cs[i * nsub + sub], 0, 0), sub=sub)) for sub in range(nsub)]
    grid_spec = pltpu.PrefetchScalarGridSpec(
        num_scalar_prefetch=2,
        grid=(nb, heads, n_rb // nsub),
        in_specs=[
            pl.BlockSpec((nsub * tq, HEAD_DIM), lambda b, h, i, cs, st: (b * qpb + i, h)),
            pl.BlockSpec((seq, HEAD_DIM), lambda b, h, i, cs, st: (b, heads + h)),
            pl.BlockSpec((seq, HEAD_DIM), lambda b, h, i, cs, st: (b, 2 * heads + h)),
            pl.BlockSpec((n_ctx, HEAD_DIM), lambda b, h, i, cs, st: (xb0 + b, heads + h)),
            pl.BlockSpec((n_ctx, HEAD_DIM), lambda b, h, i, cs, st: (xb0 + b, 2 * heads + h)),
        ] + bias_specs,
        out_specs=pl.BlockSpec((nsub * tq, HEAD_DIM), lambda b, h, i, cs, st: (b * qpb + i, h)),
    )
    return pl.pallas_call(
        functools.partial(_na_kernel, nkeys=nkeys, tq=tq),
        grid_spec=grid_spec,
        out_shape=jax.ShapeDtypeStruct((out_rows, heads * HEAD_DIM), BF16),
        compiler_params=_cp(("arbitrary",) * 3), name="neighbourhood_attention",
    )(jnp.asarray(case_of), jnp.asarray(band_r * GRID_W), p, p, p, p, p, *([bias] * nsub))


def _ctx_attn_kernel(q_ref, k_ref, v_ref, prev_ref, o_ref):
    del prev_ref
    s = lax.dot_general(q_ref[...], k_ref[...], (((1,), (1,)), ((), ())),
                        preferred_element_type=F32) * ATTN_SCALE
    m = jnp.max(s, axis=-1, keepdims=True)
    pr = jnp.exp(s - m)
    den = jnp.sum(pr, axis=-1, keepdims=True)
    o = jnp.dot(pr.astype(BF16), v_ref[...], preferred_element_type=F32)
    o_ref[...] = (o / den).astype(o_ref.dtype)


def ctx_dense_attention(p, row0, n_ctx, nb, heads, prev):
    xb0 = row0 // n_ctx
    return pl.pallas_call(
        _ctx_attn_kernel,
        grid=(nb, heads),
        in_specs=[pl.BlockSpec((n_ctx, HEAD_DIM), lambda b, h: (xb0 + b, h)),
                  pl.BlockSpec((n_ctx, HEAD_DIM), lambda b, h: (xb0 + b, heads + h)),
                  pl.BlockSpec((n_ctx, HEAD_DIM), lambda b, h: (xb0 + b, 2 * heads + h)),
                  pl.BlockSpec(memory_space=pl.ANY)],
        out_specs=pl.BlockSpec((n_ctx, HEAD_DIM), lambda b, h: (xb0 + b, h)),
        out_shape=jax.ShapeDtypeStruct(prev.shape, BF16),
        input_output_aliases={3: 0},
        compiler_params=_cp(("arbitrary", "arbitrary")), name="ctx_dense_attention",
    )(p, p, p, prev)


MOE_TM = 256


TOK_BITS = 14


def _expert_kernel(te_ref, tv_ref, code_ref, h_hbm, wgu_ref, bgu_ref, wdn_ref, bdn_ref, y_hbm,
                   xbuf, ybuf, gsem, ssem, *, de, tm):
    i = pl.program_id(0)
    slot = i % 2
    other = 1 - slot
    xs0 = lax.rem(i, 3)
    xs1 = lax.rem(i + 1, 3)
    xs2 = lax.rem(i + 2, 3)
    dh = xbuf.shape[2]

    def gather_copy(tile, sl, r):
        tok = code_ref[(tile + 2) * tm + r] & ((1 << TOK_BITS) - 1)
        return pltpu.make_async_copy(h_hbm.at[pl.ds(tok, 1)], xbuf.at[sl, pl.ds(r, 1)], gsem.at[sl])

    def scatter_copy(tile, sl, r):
        dst = lax.shift_right_logical(code_ref[(tile + 2) * tm + r], TOK_BITS)
        return pltpu.make_async_copy(ybuf.at[sl, pl.ds(r, 1)], y_hbm.at[pl.ds(dst, 1)], ssem.at[sl])

    def start_all(copy_fn, tile, sl):
        def body(r, carry):
            copy_fn(tile, sl, r).start()
            return carry
        lax.fori_loop(0, tm, body, 0, unroll=8)

    def wait_all(buf, sem, sl):
        pltpu.make_async_copy(buf.at[sl], buf.at[sl], sem.at[sl]).wait()

    @pl.when(i == 0)
    def _():
        ybuf[...] = jnp.zeros_like(ybuf)
        start_all(gather_copy, 0, 0)
        start_all(gather_copy, 1, 1)
        start_all(scatter_copy, -2, 0)

    @pl.when(tv_ref[i] != 0)
    def _():
        wait_all(xbuf, gsem, xs0)
        kc = 256
        n_kc = dh // kc
        g_per = tm // n_kc
        acc = None
        for c in range(n_kc):
            for r in range(c * g_per, (c + 1) * g_per):
                gather_copy(i + 2, xs2, r).start(priority=r % 2)
            lo, hi = _unpack_halves(xbuf[xs0, :, c * kc:(c + 1) * kc])
            xk = jnp.concatenate([lo.astype(BF16), hi.astype(BF16)], axis=1)
            wk = jnp.concatenate([wgu_ref[0, 0, c * kc:(c + 1) * kc, :].astype(BF16),
                                  wgu_ref[0, 0, dh + c * kc:dh + (c + 1) * kc, :].astype(BF16)], axis=0)
            part = jnp.dot(xk, wk, preferred_element_type=F32)
            acc = part if acc is None else acc + part
        gu = acc + bgu_ref[0, 0]
        gate = jnp.minimum(gu[:, :de], SWIGLU_LIMIT)
        up = jnp.clip(gu[:, de:], -SWIGLU_LIMIT, SWIGLU_LIMIT)
        act = (gate * _sigmoid(SWIGLU_ALPHA * gate) * (up + 1.0)).astype(BF16)
        wait_all(ybuf, ssem, slot)
        yc = 512
        n_yc = dh // yc
        s_per = tm // n_yc
        for c in range(n_yc):
            for r in range(c * s_per, (c + 1) * s_per):
                scatter_copy(i - 1, other, r).start(priority=r % 2)
            cl = slice(c * yc, (c + 1) * yc)
            ch = slice(dh + c * yc, dh + (c + 1) * yc)
            y_lo = (jnp.dot(act, wdn_ref[0, 0, :, cl].astype(BF16), preferred_element_type=F32)
                    + bdn_ref[0, 0, :, cl])
            y_hi = (jnp.dot(act, wdn_ref[0, 0, :, ch].astype(BF16), preferred_element_type=F32)
                    + bdn_ref[0, 0, :, ch])
            ybuf[slot, :, cl] = _pack_pair(y_lo.astype(BF16).astype(F32), y_hi.astype(BF16).astype(F32))

    @pl.when(jnp.logical_and(tv_ref[i] == 0, jnp.logical_and(i > 0, tv_ref[jnp.maximum(i - 1, 0)] != 0)))
    def _():
        wait_all(xbuf, gsem, xs0)
        wait_all(xbuf, gsem, xs1)
        wait_all(ybuf, ssem, slot)
        start_all(scatter_copy, i - 1, other)
        wait_all(ybuf, ssem, other)


def expert_ffn(h_packed, code, tile_e, tile_valid, w_gu, b_gu, w_dn, b_dn, layer, n_slots):
    t, dh = h_packed.shape
    d = 2 * dh
    nl, ne, _, f2 = w_gu.shape
    de = f2 // 2
    tm = MOE_TM
    nt = tile_e.shape[0]
    assert code.shape[0] == (nt + 3) * tm and tm % (dh // 256) == 0 and dh % 512 == 0
    grid_spec = pltpu.PrefetchScalarGridSpec(
        num_scalar_prefetch=3,
        grid=(nt,),
        in_specs=[
            pl.BlockSpec(memory_space=pl.ANY),
            pl.BlockSpec((1, 1, d, f2), lambda i, te, tv, cd: (layer, te[i], 0, 0)),
            pl.BlockSpec((1, 1, 1, f2), lambda i, te, tv, cd: (layer, te[i], 0, 0)),
            pl.BlockSpec((1, 1, de, d), lambda i, te, tv, cd: (layer, te[i], 0, 0)),
            pl.BlockSpec((1, 1, 1, d), lambda i, te, tv, cd: (layer, te[i], 0, 0)),
        ],
        out_specs=pl.BlockSpec(memory_space=pl.ANY),
        scratch_shapes=[pltpu.VMEM((3, tm, dh), jnp.uint32), pltpu.VMEM((2, tm, dh), jnp.uint32),
                        pltpu.SemaphoreType.DMA((3,)), pltpu.SemaphoreType.DMA((2,))],
    )
    return pl.pallas_call(
        functools.partial(_expert_kernel, de=de, tm=tm),
        grid_spec=grid_spec,
        out_shape=jax.ShapeDtypeStruct((n_slots + tm, dh), jnp.uint32),
        compiler_params=_cp(("arbitrary",)), name="expert_ffn",
    )(tile_e, tile_valid, code, h_packed, w_gu, b_gu.reshape(nl, ne, 1, f2), w_dn, b_dn.reshape(nl, ne, 1, d))


def _combine_kernel(x_ref, y0_ref, y1_ref, y2_ref, y3_ref, wt_ref, g_ref, *rest, final_norm):
    o_ref = rest[-1]
    wt = wt_ref[...]
    half = y0_ref.shape[1]
    acc_lo = jnp.zeros((x_ref.shape[0], half), F32)
    acc_hi = jnp.zeros((x_ref.shape[0], half), F32)
    for k, y_ref in enumerate((y0_ref, y1_ref, y2_ref, y3_ref)):
        lo, hi = _unpack_halves(y_ref[...])
        acc_lo = acc_lo + wt[:, k:k + 1] * lo
        acc_hi = acc_hi + wt[:, k:k + 1] * hi
    x_lo = x_ref[:, :half] + g_ref[0, :, :half] * acc_lo
    x_hi = x_ref[:, half:] + g_ref[0, :, half:] * acc_hi
    if final_norm:
        nf_ref = rest[0]
        ms = (jnp.sum(x_lo * x_lo, axis=-1, keepdims=True)
              + jnp.sum(x_hi * x_hi, axis=-1, keepdims=True)) / (2 * half)
        inv = lax.rsqrt(ms + EPS)
        x_lo = x_lo * inv * nf_ref[:, :half]
        x_hi = x_hi * inv * nf_ref[:, half:]
    o_ref[:, :half] = x_lo
    o_ref[:, half:] = x_hi


def moe_combine(x, y4, wts, g3, seg, tm, norm_final=None):
    t, d = x.shape
    tiles = t // tm
    in_specs = [pl.BlockSpec((tm, d), lambda i: (i, 0))]
    in_specs += [pl.BlockSpec((tm, d // 2), functools.partial(lambda i, k: (k * tiles + i, 0), k=k))
                 for k in range(TOP_K)]
    in_specs += [pl.BlockSpec((tm, LANES), lambda i: (i, 0)),
                 pl.BlockSpec((1, 1, d), lambda i: (seg(i), 0, 0))]
    args = [x, y4, y4, y4, y4, wts, g3]
    if norm_final is not None:
        in_specs.append(pl.BlockSpec((1, d), lambda i: (0, 0)))
        args.append(norm_final.reshape(1, d))
    return pl.pallas_call(
        functools.partial(_combine_kernel, final_norm=norm_final is not None),
        grid=(tiles,), in_specs=in_specs,
        out_specs=pl.BlockSpec((tm, d), lambda i: (i, 0)),
        out_shape=jax.ShapeDtypeStruct((t, d), F32),
        compiler_params=_cp(("arbitrary",)), name="moe_combine",
    )(*args)


def moe_layer(x, g_ffn, sc3, sh3, g3, seg, tm, w_r, b_r, w_gu, b_gu, w_dn, b_dn, layer, norm_final=None):
    t, d = x.shape
    h, topi, wts, rk, cnt = modulate(x, g_ffn, sc3, sh3, seg, tm, None, w_r, b_r)
    tme = MOE_TM
    ns = t * TOP_K
    nt = -(-(ns + N_EXPERTS * (tme - 1)) // tme) + 1
    counts = cnt[0, :N_EXPERTS]
    padded = ((counts + tme - 1) // tme) * tme
    e_i = jnp.arange(N_EXPERTS, dtype=jnp.int32)
    ends = jnp.sum(jnp.where(e_i[None, :] <= e_i[:, None], padded[None, :], 0), axis=1)
    offs = ends - padded
    ef = topi[:, :TOP_K]
    onehot = ef[:, :, None] == jnp.arange(N_EXPERTS, dtype=jnp.int32)[None, None, :]
    pos = rk[:, :TOP_K] + jnp.sum(jnp.where(onehot, offs[None, None, :], 0), axis=-1)
    assert t <= (1 << TOK_BITS) and ns + tme < (1 << (31 - TOK_BITS))
    tok = jnp.arange(t, dtype=jnp.int32)[:, None]
    slot_code = tok | ((jnp.arange(TOP_K, dtype=jnp.int32)[None, :] * t + tok) << TOK_BITS)
    pad_code = (ns + (jnp.arange((nt + 3) * tme, dtype=jnp.int32) % tme)) << TOK_BITS
    code = pad_code.at[pos.reshape(ns) + 2 * tme].set(slot_code.reshape(ns))
    tile_start = jnp.arange(nt, dtype=jnp.int32) * tme
    tile_valid = (tile_start < ends[-1]).astype(jnp.int32)
    tile_e = jnp.sum((ends[None, :] <= tile_start[:, None]).astype(jnp.int32), axis=1)
    tile_e = jnp.minimum(tile_e, N_EXPERTS - 1)
    tile_e = jnp.where(tile_valid != 0, tile_e, jnp.max(jnp.where(tile_valid != 0, tile_e, 0)))
    y4 = expert_ffn(h, code, tile_e, tile_valid, w_gu, b_gu, w_dn, b_dn, layer, ns)
    return moe_combine(x, y4, wts, g3, seg, tm, norm_final)


def kernel(x, c, ctx, c_ctx, ada_w1, ada_w2, ada_b, norm_mix, norm_ffn, ev_w_in, ev_v_norm, ev_w_s, ev_b_s, ev_w_out, od_w_in, od_rpb, od_lam, od_sub_norm, od_w_out, moe_w_r, moe_b_r, moe_w_gu, moe_b_gu, moe_w_dn, moe_b_dn, norm_final):
    nb, seq, d = x.shape
    n_ctx = ctx.shape[1]
    depth = ada_w1.shape[0]
    n_lat = nb * seq
    t_all = n_lat + nb * n_ctx
    tm = 512 if (t_all % 512 == 0 and seq % 512 == 0) else 256
    tms = 256
    assert n_lat % tm == 0 and (nb * n_ctx) % tms == 0 and seq % tms == 0 and nb + 1 <= 8

    def seg_for(tile):
        return _seg_fn(n_lat // tile, seq // tile, nb)

    ev_w_in_b = cast_bf16(ev_w_in, 256)
    ev_w_out_b = cast_bf16(ev_w_out, 512)
    od_w_in_b = cast_bf16(od_w_in, 128)
    od_w_out_b = cast_bf16(od_w_out, 512)

    xs = jnp.concatenate([x.reshape(n_lat, d), ctx.reshape(nb * n_ctx, d)], axis=0)

    cv = jnp.zeros((8, d), F32).at[:nb].set(c).at[nb].set(c_ctx)
    mods = ada_all(cv, ada_w1, ada_w2, ada_b)

    for i in range(depth):
        last = i == depth - 1
        j = i // 2
        m6 = mods[i].reshape(8, 6, d)
        sh1, sc1, g1, sh2, sc2, g2 = [m6[:, q, :].reshape(8, 1, d) for q in range(6)]
        h = modulate(xs, norm_mix[i], sc1, sh1, seg_for(tms), tms)
        n_out = n_lat if last else t_all
        if i % 2 == 0:
            p = matmul(h, ev_w_in_b, j, tm)
            y1 = even_a(p, ev_v_norm[j], ev_w_s[j], ev_b_s[j], tms)
            y2 = even_b(p, seq, n_ctx, nb, tms)
            w_out = ev_w_out_b
        else:
            lambda_init = 0.8 - 0.6 * math.exp(-0.3 * i)
            p = matmul(h, od_w_in_b, j, tm)
            cw = p.shape[1] // 6
            heads_c = cw // HEAD_DIM
            heads_d = cw // (2 * HEAD_DIM)
            y1 = neighbourhood_attention(p, od_rpb[j], seq, n_ctx, nb, heads_c, n_out)
            qk = rope_qk(p, seq, nb, tms)
            nblk = cw // (2 * HEAD_DIM)
            seg_x = (p, n_lat, 4 * nblk, p, n_lat, 5 * nblk, n_ctx)
            seg_l = (qk, 0, nblk, p, 0, 5 * nblk, seq)
            y2 = diff_attention(qk, 0, 0, seq, [seg_x, seg_l], od_lam[j], od_sub_norm[j], nb, heads_d,
                                lambda_init, 256, n_out, 0)
            if not last:
                y1 = ctx_dense_attention(p, n_lat, n_ctx, nb, heads_c, y1)
                y2 = diff_attention(p, n_lat, 3 * nblk, n_ctx, [seg_x], od_lam[j], od_sub_norm[j], nb,
                                    heads_d, lambda_init, 256, n_out, n_lat, prev=y2)
            w_out = od_w_out_b
        xs = matmul_residual(y1, y2, w_out, j, xs, g1, seg_for(tm), tm, n_out)
        xs = moe_layer(xs, norm_ffn[i], sc2, sh2, g2, seg_for(tms), tms, moe_w_r[i], moe_b_r[i],
                       moe_w_gu, moe_b_gu, moe_w_dn, moe_b_dn, i, norm_final if last else None)
    return xs.reshape(nb, seq, d)
```

```python
import functools
import math

import numpy as np
import jax
import jax.numpy as jnp
from jax import lax
from jax.experimental import pallas as pl
from jax.experimental.pallas import tpu as pltpu

F32 = jnp.float32
BF16 = jnp.bfloat16

GRID_W = 64
HEAD_DIM = 128
EPS = 1e-6
ROPE_THETA = 10000.0
CHUNK = 128
NA_KH = 8
NA_KW = 16
NA_QR = 8
ATTN_SCALE = HEAD_DIM ** -0.5
NEG = -1e30
N_EXPERTS = 32
TOP_K = 4
SWIGLU_LIMIT = 7.0
SWIGLU_ALPHA = 1.702
LANES = 128

VMEM_LIMIT = 56 * 1024 * 1024


def _cp(sem, vmem=VMEM_LIMIT):
    return pltpu.CompilerParams(dimension_semantics=sem, vmem_limit_bytes=vmem)


def _gelu_tanh(x):
    return 0.5 * x * (1.0 + jnp.tanh(0.7978845608028654 * (x + 0.044715 * (x * x * x))))


def _sigmoid(x):
    return 1.0 / (1.0 + jnp.exp(-x))


def _pack_halves(x):
    c = x.shape[1] // 2
    return _pack_pair(x[:, :c], x[:, c:])


def _pack_pair(left, right):
    lo = lax.shift_right_logical(lax.bitcast_convert_type(left, jnp.uint32), jnp.uint32(16))
    hi = lax.bitcast_convert_type(right, jnp.uint32) & jnp.uint32(0xFFFF0000)
    return lo | hi


def _unpack_halves(w):
    lo = lax.bitcast_convert_type(lax.shift_left(w, jnp.uint32(16)), F32)
    hi = lax.bitcast_convert_type(w & jnp.uint32(0xFFFF0000), F32)
    return lo, hi


def _pick(n, prefs):
    for p in prefs:
        if n % p == 0:
            return p
    raise ValueError(f"no tile for {n}")


def _ada_kernel(cv_ref, w1_ref, w2_ref, b_ref, o_ref):
    cv = cv_ref[...]
    s = cv * _sigmoid(cv)
    t = jnp.dot(s, w1_ref[0], preferred_element_type=F32, precision=lax.Precision.HIGHEST)
    m = jnp.dot(t, w2_ref[0], preferred_element_type=F32, precision=lax.Precision.HIGHEST)
    o_ref[0] = m + b_ref[0]


def ada_all(cv, w1, w2, b):
    nl, d, r = w1.shape
    n6 = w2.shape[2]
    tn = 2048
    return pl.pallas_call(
        _ada_kernel,
        grid=(nl, n6 // tn),
        in_specs=[
            pl.BlockSpec((8, d), lambda l, j: (0, 0)),
            pl.BlockSpec((1, d, r), lambda l, j: (l, 0, 0)),
            pl.BlockSpec((1, r, tn), lambda l, j: (l, 0, j)),
            pl.BlockSpec((1, 1, tn), lambda l, j: (l, 0, j)),
        ],
        out_specs=pl.BlockSpec((1, 8, tn), lambda l, j: (l, 0, j)),
        out_shape=jax.ShapeDtypeStruct((nl, 8, n6), F32),
        compiler_params=_cp(("arbitrary", "arbitrary")),
        name="ada_mod",
    )(cv, w1, w2, b.reshape(nl, 1, n6))


def _seg_fn(n_lat_tiles, tiles_per_batch, n_batch):
    def seg(i):
        return jnp.where(i < n_lat_tiles, i // tiles_per_batch, n_batch)
    return seg


def _mod_kernel(x_ref, g_ref, sc_ref, sh_ref, o_ref):
    x = x_ref[...]
    ms = jnp.mean(x * x, axis=-1, keepdims=True)
    y = x * lax.rsqrt(ms + EPS) * g_ref[...]
    o_ref[...] = (y * (1.0 + sc_ref[0]) + sh_ref[0]).astype(o_ref.dtype)


def _mod_router_kernel(x_ref, g_ref, sc_ref, sh_ref, wr_ref, br_ref, o_ref, ti_ref, wt_ref, rk_ref, cnt_ref,
                       base_ref):
    @pl.when(pl.program_id(0) == 0)
    def _():
        base_ref[...] = jnp.zeros_like(base_ref)

    x = x_ref[...]
    ms = jnp.mean(x * x, axis=-1, keepdims=True)
    y = x * lax.rsqrt(ms + EPS) * g_ref[...]
    h = y * (1.0 + sc_ref[0]) + sh_ref[0]
    h_hi = h.astype(BF16)
    o_ref[...] = _pack_halves(h_hi.astype(F32))
    h_lo = (h - h_hi.astype(F32)).astype(BF16)
    w = wr_ref[...]
    w_hi = w.astype(BF16)
    w_lo = (w - w_hi.astype(F32)).astype(BF16)
    logits = (jnp.dot(h_hi, w_hi, preferred_element_type=F32)
              + jnp.dot(h_hi, w_lo, preferred_element_type=F32)
              + jnp.dot(h_lo, w_hi, preferred_element_type=F32)) + br_ref[...]
    tm = logits.shape[0]
    lane = lax.broadcasted_iota(jnp.int32, logits.shape, 1).astype(F32)
    vals, idxs = [], []
    cur = logits
    for _ in range(TOP_K):
        m = jnp.max(cur, axis=-1, keepdims=True)
        idx = jnp.min(jnp.where(cur == m, lane, float(LANES)), axis=-1, keepdims=True)
        vals.append(m)
        idxs.append(idx)
        cur = jnp.where(lane == idx, -jnp.inf, cur)
    es = [jnp.exp(v - vals[0]) for v in vals]
    den = es[0] + es[1] + es[2] + es[3]
    sel = [lane == idxs[k] for k in range(TOP_K)]
    osum = jnp.zeros(logits.shape, F32)
    for k in range(TOP_K):
        osum = osum + jnp.where(sel[k], 1.0, 0.0)
    r_i = lax.broadcasted_iota(jnp.int32, (tm, tm), 0)
    c_i = lax.broadcasted_iota(jnp.int32, (tm, tm), 1)
    lower = jnp.where(c_i < r_i, 1.0, 0.0).astype(BF16)
    pref = jnp.dot(lower, osum.astype(BF16), preferred_element_type=F32) + base_ref[...]
    ti = jnp.zeros(logits.shape, F32)
    wt = jnp.zeros(logits.shape, F32)
    rk = jnp.zeros(logits.shape, F32)
    for k in range(TOP_K):
        rank_k = jnp.sum(jnp.where(sel[k], pref, 0.0), axis=-1, keepdims=True)
        ti = jnp.where(lane == float(k), idxs[k], ti)
        wt = jnp.where(lane == float(k), es[k] / den, wt)
        rk = jnp.where(lane == float(k), rank_k, rk)
    ti_ref[...] = ti.astype(jnp.int32)
    wt_ref[...] = wt
    rk_ref[...] = rk.astype(jnp.int32)
    base_ref[...] += jnp.sum(osum, axis=0, keepdims=True)
    cnt_ref[...] = base_ref[...].astype(jnp.int32)


def modulate(x, g, sc3, sh3, seg, tm, m_rows=None, w_r=None, b_r=None):
    t, d = x.shape
    t = t if m_rows is None else m_rows
    in_specs = [
        pl.BlockSpec((tm, d), lambda i: (i, 0)),
        pl.BlockSpec((1, d), lambda i: (0, 0)),
        pl.BlockSpec((1, 1, d), lambda i: (seg(i), 0, 0)),
        pl.BlockSpec((1, 1, d), lambda i: (seg(i), 0, 0)),
    ]
    if w_r is None:
        return pl.pallas_call(
            _mod_kernel, grid=(t // tm,), in_specs=in_specs,
            out_specs=pl.BlockSpec((tm, d), lambda i: (i, 0)),
            out_shape=jax.ShapeDtypeStruct((t, d), BF16),
            compiler_params=_cp(("arbitrary",)), name="modulate",
        )(x, g.reshape(1, d), sc3, sh3)
    wr_pad = jnp.zeros((d, LANES), F32).at[:, :N_EXPERTS].set(w_r)
    br_pad = jnp.full((1, LANES), NEG, F32).at[0, :N_EXPERTS].set(b_r)
    in_specs += [pl.BlockSpec((d, LANES), lambda i: (0, 0)), pl.BlockSpec((1, LANES), lambda i: (0, 0))]
    row_spec = pl.BlockSpec((tm, LANES), lambda i: (i, 0))
    return pl.pallas_call(
        _mod_router_kernel, grid=(t // tm,), in_specs=in_specs,
        out_specs=[pl.BlockSpec((tm, d // 2), lambda i: (i, 0)), row_spec, row_spec, row_spec,
                   pl.BlockSpec((1, LANES), lambda i: (0, 0))],
        out_shape=[jax.ShapeDtypeStruct((t, d // 2), jnp.uint32),
                   jax.ShapeDtypeStruct((t, LANES), jnp.int32),
                   jax.ShapeDtypeStruct((t, LANES), F32),
                   jax.ShapeDtypeStruct((t, LANES), jnp.int32),
                   jax.ShapeDtypeStruct((1, LANES), jnp.int32)],
        scratch_shapes=[pltpu.VMEM((1, LANES), F32)],
        compiler_params=_cp(("arbitrary",)), name="modulate_router",
    )(x, g.reshape(1, d), sc3, sh3, wr_pad, br_pad)


def _cast_kernel(x_ref, o_ref):
    o_ref[...] = x_ref[...].astype(o_ref.dtype)


def cast_bf16(w, block_rows):
    shape = w.shape
    w2 = w.reshape(-1, shape[-1])
    r, c = w2.shape
    out = pl.pallas_call(
        _cast_kernel, grid=(r // block_rows,),
        in_specs=[pl.BlockSpec((block_rows, c), lambda i: (i, 0))],
        out_specs=pl.BlockSpec((block_rows, c), lambda i: (i, 0)),
        out_shape=jax.ShapeDtypeStruct((r, c), BF16),
        compiler_params=_cp(("arbitrary",)), name="cast_bf16",
    )(w2)
    return out.reshape(shape)


def _mm_kernel(a_ref, b_ref, o_ref, wb_ref):
    @pl.when(pl.program_id(1) == 0)
    def _():
        wb_ref[...] = b_ref[0].astype(BF16)

    o_ref[...] = jnp.dot(a_ref[...], wb_ref[...], preferred_element_type=F32).astype(o_ref.dtype)


def matmul(a, b3, layer, tm, out_dtype=BF16, tn=1024):
    m, kk = a.shape
    n = b3.shape[2]
    tn = _pick(n, (tn, 512, 256))
    return pl.pallas_call(
        _mm_kernel,
        grid=(n // tn, m // tm),
        in_specs=[pl.BlockSpec((tm, kk), lambda j, i: (i, 0)),
                  pl.BlockSpec((1, kk, tn), lambda j, i: (layer, 0, j))],
        out_specs=pl.BlockSpec((tm, tn), lambda j, i: (i, j)),
        out_shape=jax.ShapeDtypeStruct((m, n), out_dtype),
        scratch_shapes=[pltpu.VMEM((kk, tn), BF16)],
        compiler_params=_cp(("arbitrary", "arbitrary")), name="matmul",
    )(a, b3)


def _mm_res_kernel(a1_ref, a2_ref, b_ref, x_ref, g_ref, o_ref):
    k1 = a1_ref.shape[1]
    acc = jnp.dot(a1_ref[...], b_ref[0, :k1, :], preferred_element_type=F32)
    acc = acc + jnp.dot(a2_ref[...], b_ref[0, k1:, :], preferred_element_type=F32)
    o_ref[...] = x_ref[...] + g_ref[0] * acc


def matmul_residual(a1, a2, b3, layer, x, g3, seg, tm, m_rows, tn=1024):
    k1, k2 = a1.shape[1], a2.shape[1]
    n = b3.shape[2]
    tn = _pick(n, (tn, 512, 256))
    return pl.pallas_call(
        _mm_res_kernel,
        grid=(n // tn, m_rows // tm),
        in_specs=[pl.BlockSpec((tm, k1), lambda j, i: (i, 0)),
                  pl.BlockSpec((tm, k2), lambda j, i: (i, 0)),
                  pl.BlockSpec((1, k1 + k2, tn), lambda j, i: (layer, 0, j)),
                  pl.BlockSpec((tm, tn), lambda j, i: (i, j)),
                  pl.BlockSpec((1, 1, tn), lambda j, i: (seg(i), 0, j))],
        out_specs=pl.BlockSpec((tm, tn), lambda j, i: (i, j)),
        out_shape=jax.ShapeDtypeStruct((m_rows, n), F32),
        compiler_params=_cp(("arbitrary", "arbitrary")), name="matmul_residual",
    )(a1, a2, b3, x, g3)


def _bmm_kernel(a_ref, b1_ref, b2_ref, *rest, nkh):
    o_ref, acc_ref = rest[-2], rest[-1]
    k = pl.program_id(3)

    @pl.when(k == 0)
    def _():
        acc_ref[...] = jnp.zeros_like(acc_ref)

    @pl.when(k < nkh)
    def _():
        acc_ref[...] += jnp.dot(a_ref[...], b1_ref[0], preferred_element_type=F32)

    @pl.when(k >= nkh)
    def _():
        acc_ref[...] += jnp.dot(a_ref[...], b2_ref[0], preferred_element_type=F32)

    @pl.when(k == 2 * nkh - 1)
    def _():
        o_ref[...] = acc_ref[...].astype(o_ref.dtype)


def shared_lhs_bmm(a, b1, b2, out_rows, row0, prev=None):
    m, kk2 = a.shape
    nb, kk, n = b1.shape
    tm, tn, tk = _pick(m, (512, 256, 128)), _pick(n, (1024, 512)), _pick(kk, (4096, 2048, 1024, 512, 256, 128))
    nkh = kk // tk
    nk = 2 * nkh
    r0 = row0 // tm
    mt = m // tm
    in_specs = [pl.BlockSpec((tm, tk), lambda z, j, i, k: (i, k)),
                pl.BlockSpec((1, tk, tn), lambda z, j, i, k: (z, jnp.minimum(k, nkh - 1), j)),
                pl.BlockSpec((1, tk, tn), lambda z, j, i, k: (z, jnp.maximum(k - nkh, 0), j))]
    args = [a, b1, b2]
    aliases = {}
    if prev is not None:
        in_specs.append(pl.BlockSpec(memory_space=pl.ANY))
        args.append(prev)
        aliases = {3: 0}
    return pl.pallas_call(
        functools.partial(_bmm_kernel, nkh=nkh),
        grid=(nb, n // tn, mt, nk),
        in_specs=in_specs,
        out_specs=pl.BlockSpec((tm, tn), lambda z, j, i, k: (r0 + z * mt + i, j)),
        out_shape=jax.ShapeDtypeStruct((out_rows, n), BF16),
        scratch_shapes=[pltpu.VMEM((tm, tn), F32)],
        input_output_aliases=aliases,
        compiler_params=_cp(("arbitrary",) * 4), name="dft_seq_matmul",
    )(*args)


def _even_a_kernel(u_ref, v_ref, gv_ref, ws_ref, bs_ref, o_ref, *, tm, heads):
    for h in range(heads):
        cs = slice(h * HEAD_DIM, (h + 1) * HEAD_DIM)
        u = _gelu_tanh(u_ref[:, cs].astype(F32))
        v = _gelu_tanh(v_ref[:, cs].astype(F32))
        v = v * lax.rsqrt(jnp.mean(v * v, axis=-1, keepdims=True) + EPS) * gv_ref[:, cs]
        vb = v.astype(BF16)
        w = ws_ref[h]
        bias = bs_ref[h]
        for c in range(tm // CHUNK):
            rs = slice(c * CHUNK, (c + 1) * CHUNK)
            mixed = jnp.dot(w, vb[rs], preferred_element_type=F32) + bias
            o_ref[rs, cs] = (u[rs] * mixed).astype(o_ref.dtype)


def even_a(p, g_v, w_s, b_s, tm):
    t = p.shape[0]
    heads = w_s.shape[0]
    aw = heads * HEAD_DIM
    bsb = jnp.broadcast_to(b_s[:, :, None], (heads, CHUNK, HEAD_DIM)).astype(F32)
    return pl.pallas_call(
        functools.partial(_even_a_kernel, tm=tm, heads=heads),
        grid=(t // tm,),
        in_specs=[pl.BlockSpec((tm, aw), lambda i: (i, 0)),
                  pl.BlockSpec((tm, aw), lambda i: (i, 1)),
                  pl.BlockSpec((1, aw), lambda i: (0, 0)),
                  pl.BlockSpec((heads, CHUNK, CHUNK), lambda i: (0, 0, 0)),
                  pl.BlockSpec((heads, CHUNK, HEAD_DIM), lambda i: (0, 0, 0))],
        out_specs=pl.BlockSpec((tm, aw), lambda i: (i, 0)),
        out_shape=jax.ShapeDtypeStruct((t, aw), BF16),
        compiler_params=_cp(("arbitrary",)), name="even_chunk_gmlp",
    )(p, p, g_v.reshape(1, aw), w_s.astype(BF16), bsb)


def _even_b1_kernel(z_ref, cs_ref, oc_ref, os_ref, *, groups):
    for g in range(groups):
        cs = slice(g * HEAD_DIM, (g + 1) * HEAD_DIM)
        r = jnp.dot(z_ref[:, cs], cs_ref[...], preferred_element_type=F32)
        oc_ref[0, :, cs] = r[:, :HEAD_DIM].astype(oc_ref.dtype)
        os_ref[0, :, cs] = r[:, HEAD_DIM:].astype(os_ref.dtype)


def even_b_feature_dft(p, cs_mat, row0, n, nb, tm):
    bw = p.shape[1] // 3
    groups = bw // HEAD_DIM
    tpb = n // tm
    t0 = row0 // tm
    out_spec = pl.BlockSpec((1, tm, bw), lambda b, j: (b, j, 0))
    return pl.pallas_call(
        functools.partial(_even_b1_kernel, groups=groups),
        grid=(nb, tpb),
        in_specs=[pl.BlockSpec((tm, bw), lambda b, j: (t0 + b * tpb + j, 2)),
                  pl.BlockSpec((HEAD_DIM, 2 * HEAD_DIM), lambda b, j: (0, 0))],
        out_specs=[out_spec, out_spec],
        out_shape=[jax.ShapeDtypeStruct((nb, n, bw), BF16), jax.ShapeDtypeStruct((nb, n, bw), BF16)],
        compiler_params=_cp(("arbitrary", "arbitrary")), name="dft_feature",
    )(p, cs_mat)


def _dft_mats(n):
    k = jnp.arange(n, dtype=jnp.int32)
    s = 1.0 / math.sqrt(n)
    b = 64
    if n % b or n <= b:
        ang = ((k[:, None] * k[None, :]) % n).astype(F32) * (2.0 * math.pi / n)
        return jnp.cos(ang) * s, jnp.sin(ang) * s
    a = n // b
    ang_a = ((k[:, None] * (jnp.arange(a, dtype=jnp.int32) * b)[None, :]) % n).astype(F32) * (2.0 * math.pi / n)
    ang_b = ((k[:, None] * jnp.arange(b, dtype=jnp.int32)[None, :]) % n).astype(F32) * (2.0 * math.pi / n)
    ca, sa = jnp.cos(ang_a)[:, :, None] * s, jnp.sin(ang_a)[:, :, None] * s
    cb, sb = jnp.cos(ang_b)[:, None, :], jnp.sin(ang_b)[:, None, :]
    return (ca * cb - sa * sb).reshape(n, n), (sa * cb + ca * sb).reshape(n, n)


def dft_constants(n_lat, n_ctx):
    c_d, s_d = _dft_mats(HEAD_DIM)
    cs_mat = jnp.concatenate([c_d, s_d], axis=1).astype(BF16)
    fcats = []
    for n in (n_lat, n_ctx):
        c_n, s_n = _dft_mats(n)
        fcats.append(jnp.concatenate([c_n, -s_n], axis=1).astype(BF16))
    return cs_mat, fcats


def even_b(p, consts, n_lat, n_ctx, nb, tm):
    cs_mat, fcats = consts
    t_all = p.shape[0]
    y = None
    for fcat, row0, n in ((fcats[0], 0, n_lat), (fcats[1], nb * n_lat, n_ctx)):
        zc, zs = even_b_feature_dft(p, cs_mat, row0, n, nb, min(tm, n))
        y = shared_lhs_bmm(fcat, zc, zs, t_all, row0, prev=y)
    return y


def _rope_kernel(q_ref, k_ref, cos_ref, sin_ref, o_ref, *, nblk):
    c = cos_ref[...]
    s = sin_ref[...]
    for src, base in ((q_ref, 0), (k_ref, nblk)):
        for j in range(nblk):
            t = src[:, j * HEAD_DIM:(j + 1) * HEAD_DIM].astype(F32)
            r = t * c + pltpu.roll(t, HEAD_DIM // 2, 1) * s
            o_ref[:, (base + j) * HEAD_DIM:(base + j + 1) * HEAD_DIM] = r.astype(o_ref.dtype)


def rope_qk(p, seq, nb, tm):
    dw = p.shape[1] // 6
    pos = jnp.arange(seq)
    row = (pos // GRID_W).astype(F32)
    col = (pos % GRID_W).astype(F32)
    nf = HEAD_DIM // 4
    inv = ROPE_THETA ** (-jnp.arange(nf, dtype=F32) / nf)
    ang = jnp.concatenate([row[:, None] * inv, col[:, None] * inv], axis=-1)
    cos, sin = jnp.cos(ang), jnp.sin(ang)
    cos2 = jnp.concatenate([cos, cos], axis=-1)
    sin2 = jnp.concatenate([-sin, sin], axis=-1)
    tpb = seq // tm
    return pl.pallas_call(
        functools.partial(_rope_kernel, nblk=dw // HEAD_DIM),
        grid=(nb * tpb,),
        in_specs=[pl.BlockSpec((tm, dw), lambda i: (i, 3)),
                  pl.BlockSpec((tm, dw), lambda i: (i, 4)),
                  pl.BlockSpec((tm, HEAD_DIM), lambda i: (i % tpb, 0)),
                  pl.BlockSpec((tm, HEAD_DIM), lambda i: (i % tpb, 0))],
        out_specs=pl.BlockSpec((tm, 2 * dw), lambda i: (i, 0)),
        out_shape=jax.ShapeDtypeStruct((nb * seq, 2 * dw), BF16),
        compiler_params=_cp(("arbitrary",)), name="rope",
    )(p, p, cos2, sin2)


def _diff_attn_kernel(*refs, seg_lens, kc, lambda_init):
    nseg = len(seg_lens)
    q_ref = refs[0]
    lam_ref, g_ref = refs[1 + 2 * nseg], refs[2 + 2 * nseg]
    o_ref = refs[-1]
    lp = lam_ref[...]
    lam = (jnp.exp(jnp.sum(lp[0:1] * lp[1:2], axis=-1, keepdims=True))
           - jnp.exp(jnp.sum(lp[2:3] * lp[3:4], axis=-1, keepdims=True)) + lambda_init)
    tq = q_ref.shape[0]
    dv = o_ref.shape[1]
    qs = (q_ref[:, :HEAD_DIM], q_ref[:, HEAD_DIM:])
    chunks = []
    for si in range(nseg):
        n = seg_lens[si]
        step = min(kc, n)
        chunks += [(refs[1 + 2 * si], refs[2 + 2 * si], c0, step) for c0 in range(0, n, step)]

    def scores(t, k_ref, c0, step):
        return lax.dot_general(qs[t], k_ref[c0:c0 + step, t * HEAD_DIM:(t + 1) * HEAD_DIM],
                               (((1,), (1,)), ((), ())), preferred_element_type=F32)

    def fold(x, op):
        r = x[:, :LANES]
        for j in range(1, x.shape[1] // LANES):
            r = op(r, x[:, j * LANES:(j + 1) * LANES])
        return r

    mx = [None, None]
    for (k_ref, _, c0, step) in chunks:
        for t in range(2):
            f = fold(scores(t, k_ref, c0, step), jnp.maximum)
            mx[t] = f if mx[t] is None else jnp.maximum(mx[t], f)
    c_exp = ATTN_SCALE * math.log2(math.e)
    mc = [jnp.max(mx[t], axis=-1, keepdims=True) * c_exp for t in range(2)]
    lsum = [None, None]
    acc = [None, None]
    for (k_ref, v_ref, c0, step) in chunks:
        vblk = v_ref[c0:c0 + step, :]
        for t in range(2):
            pr = jnp.exp2(scores(t, k_ref, c0, step) * c_exp - mc[t])
            f = fold(pr, jnp.add)
            lsum[t] = f if lsum[t] is None else lsum[t] + f
            pv = jnp.dot(pr.astype(BF16), vblk, preferred_element_type=F32)
            acc[t] = pv if acc[t] is None else acc[t] + pv
    l = [jnp.sum(lsum[t], axis=-1, keepdims=True) for t in range(2)]
    o = acc[0] / l[0] - lam * (acc[1] / l[1])
    o = o * lax.rsqrt(jnp.mean(o * o, axis=-1, keepdims=True) + EPS) * g_ref[...]
    o_ref[...] = (o * (1.0 - lambda_init)).astype(o_ref.dtype)


def diff_attention(q_arr, q_row0, q_col0, nq, segs, lam_p, g_sub, nb, heads, lambda_init, tq,
                   out_rows, out_row0, prev=None):
    dv = 2 * HEAD_DIM
    tq = min(tq, nq)
    qpb = nq // tq
    in_specs = [pl.BlockSpec((tq, dv), lambda b, h, i: (q_row0 // tq + b * qpb + i, q_col0 + h))]
    args = [q_arr]
    seg_lens = []
    for (k_arr, k_row0, k_col0, v_arr, v_row0, v_col0, n) in segs:
        in_specs.append(pl.BlockSpec((n, dv), functools.partial(
            lambda b, h, i, r0, c0, nn: (r0 // nn + b, c0 + h), r0=k_row0, c0=k_col0, nn=n)))
        in_specs.append(pl.BlockSpec((n, dv), functools.partial(
            lambda b, h, i, r0, c0, nn: (r0 // nn + b, c0 + h), r0=v_row0, c0=v_col0, nn=n)))
        args += [k_arr, v_arr]
        seg_lens.append(n)
    in_specs += [pl.BlockSpec((4, HEAD_DIM), lambda b, h, i: (0, 0)),
                 pl.BlockSpec((1, dv), lambda b, h, i: (0, 0))]
    args += [lam_p.astype(F32), g_sub.reshape(1, dv)]
    aliases = {}
    if prev is not None:
        in_specs.append(pl.BlockSpec(memory_space=pl.ANY))
        aliases = {len(args): 0}
        args.append(prev)
    o0 = out_row0 // tq
    return pl.pallas_call(
        functools.partial(_diff_attn_kernel, seg_lens=tuple(seg_lens), kc=1024, lambda_init=lambda_init),
        grid=(nb, heads, qpb),
        in_specs=in_specs,
        out_specs=pl.BlockSpec((tq, dv), lambda b, h, i: (o0 + b * qpb + i, h)),
        out_shape=jax.ShapeDtypeStruct((out_rows, heads * dv), BF16),
        input_output_aliases=aliases,
        compiler_params=_cp(("arbitrary",) * 3), name="diff_attention",
    )(*args)


def _na_geometry(rows):
    kh = min(NA_KH, rows)
    qr = math.gcd(rows, NA_QR)
    kb = min(kh + qr - 1, rows)
    n_rb = rows // qr
    band_r = np.clip(np.arange(n_rb) * qr - kh // 2, 0, rows - kb)
    return kh, qr, kb, n_rb, band_r


def _na_bias(rpb, rows):
    kh, qr, kb, n_rb, band_r = _na_geometry(rows)
    nh = rpb.shape[0]
    rel = band_r - np.arange(n_rb) * qr
    cases, case_of = [], []
    for i in range(n_rb):
        q_row = i * qr + np.arange(qr)
        win_r = np.clip(q_row - kh // 2, 0, rows - kh)
        key = (int(rel[i]), tuple((win_r - i * qr).tolist()))
        if key not in cases:
            cases.append(key)
        case_of.append(cases.index(key))
    q_col = np.arange(GRID_W)
    key_c = np.arange(GRID_W)
    win_c = np.clip(q_col - NA_KW // 2, 0, GRID_W - NA_KW)
    ok_c = (key_c[None, :] >= win_c[:, None]) & (key_c[None, :] < win_c[:, None] + NA_KW)
    dc = np.clip(key_c[None, :] - q_col[:, None], 1 - NA_KW, NA_KW - 1) + NA_KW - 1
    onehot = np.zeros((2 * NA_KW - 1, GRID_W, GRID_W), np.float32)
    onehot[dc, q_col[:, None], key_c[None, :]] = 1.0
    col_tab = jnp.einsum("hab,bck->hack", rpb.astype(F32), jnp.asarray(onehot),
                         precision=lax.Precision.HIGHEST)
    col_tab = jnp.where(jnp.asarray(ok_c)[None, None], col_tab, NEG)
    tabs = []
    for (r, win_rel) in cases:
        q_row = np.arange(qr)
        key_r = r + np.arange(kb)
        win_r = np.asarray(win_rel)
        ok_r = (key_r[None, :] >= win_r[:, None]) & (key_r[None, :] < win_r[:, None] + kh)
        dr = np.clip(key_r[None, :] - q_row[:, None], 1 - NA_KH, NA_KH - 1) + NA_KH - 1
        blk = col_tab[:, dr.reshape(-1)].reshape(nh, qr, kb, GRID_W, GRID_W)
        blk = jnp.where(jnp.asarray(ok_r)[None, :, :, None, None], blk, NEG)
        tabs.append(blk.transpose(0, 1, 3, 2, 4).reshape(nh, qr * GRID_W, kb * GRID_W).astype(BF16))
    return jnp.stack(tabs, axis=1), np.asarray(case_of, np.int32), band_r.astype(np.int32)


NA_BLOCKS_PER_STEP = 2


def _na_kernel(case_ref, start_ref, q_ref, k_ref, v_ref, kx_ref, vx_ref, *rest, nkeys, tq):
    bias_refs, o_ref = rest[:-1], rest[-1]
    i = pl.program_id(2)
    dn = (((1,), (1,)), ((), ()))
    for sub, bias_ref in enumerate(bias_refs):
        start = pl.multiple_of(start_ref[i * len(bias_refs) + sub], GRID_W)
        rs = slice(sub * tq, (sub + 1) * tq)
        q = q_ref[rs, :]
        kb = k_ref[pl.ds(start, nkeys), :]
        vb = v_ref[pl.ds(start, nkeys), :]
        s_loc = lax.dot_general(q, kb, dn, preferred_element_type=F32) * ATTN_SCALE + bias_ref[0, 0].astype(F32)
        s_ctx = lax.dot_general(q, kx_ref[...], dn, preferred_element_type=F32) * ATTN_SCALE
        m = jnp.maximum(jnp.max(s_loc, axis=-1, keepdims=True), jnp.max(s_ctx, axis=-1, keepdims=True))
        p_loc = jnp.exp(s_loc - m)
        p_ctx = jnp.exp(s_ctx - m)
        den = jnp.sum(p_loc, axis=-1, keepdims=True) + jnp.sum(p_ctx, axis=-1, keepdims=True)
        o = (jnp.dot(p_loc.astype(BF16), vb, preferred_element_type=F32)
             + jnp.dot(p_ctx.astype(BF16), vx_ref[...], preferred_element_type=F32))
        o_ref[rs, :] = (o / den).astype(o_ref.dtype)


def neighbourhood_attention(p, rpb, seq, n_ctx, nb, heads, out_rows):
    rows = seq // GRID_W
    kh, qr, kb, n_rb, _ = _na_geometry(rows)
    bias, case_of, band_r = _na_bias(rpb, rows)
    tq = qr * GRID_W
    nkeys = kb * GRID_W
    nsub = NA_BLOCKS_PER_STEP if n_rb % NA_BLOCKS_PER_STEP == 0 else 1
    qpb = seq // (nsub * tq)
    xb0 = nb * seq // n_ctx
    bias_specs = [pl.BlockSpec((1, 1, tq, nkeys), functools.partial(
        lambda b, h, i, cs, st, sub: (h, cs[i * nsub + sub], 0, 0), sub=sub)) for sub in range(nsub)]
    grid_spec = pltpu.PrefetchScalarGridSpec(
        num_scalar_prefetch=2,
        grid=(nb, heads, n_rb // nsub),
        in_specs=[
            pl.BlockSpec((nsub * tq, HEAD_DIM), lambda b, h, i, cs, st: (b * qpb + i, h)),
            pl.BlockSpec((seq, HEAD_DIM), lambda b, h, i, cs, st: (b, heads + h)),
            pl.BlockSpec((seq, HEAD_DIM), lambda b, h, i, cs, st: (b, 2 * heads + h)),
            pl.BlockSpec((n_ctx, HEAD_DIM), lambda b, h, i, cs, st: (xb0 + b, heads + h)),
            pl.BlockSpec((n_ctx, HEAD_DIM), lambda b, h, i, cs, st: (xb0 + b, 2 * heads + h)),
        ] + bias_specs,
        out_specs=pl.BlockSpec((nsub * tq, HEAD_DIM), lambda b, h, i, cs, st: (b * qpb + i, h)),
    )
    return pl.pallas_call(
        functools.partial(_na_kernel, nkeys=nkeys, tq=tq),
        grid_spec=grid_spec,
        out_shape=jax.ShapeDtypeStruct((out_rows, heads * HEAD_DIM), BF16),
        compiler_params=_cp(("arbitrary",) * 3), name="neighbourhood_attention",
    )(jnp.asarray(case_of), jnp.asarray(band_r * GRID_W), p, p, p, p, p, *([bias] * nsub))


def _ctx_attn_kernel(q_ref, k_ref, v_ref, prev_ref, o_ref):
    del prev_ref
    s = lax.dot_general(q_ref[...], k_ref[...], (((1,), (1,)), ((), ())),
                        preferred_element_type=F32) * ATTN_SCALE
    m = jnp.max(s, axis=-1, keepdims=True)
    pr = jnp.exp(s - m)
    den = jnp.sum(pr, axis=-1, keepdims=True)
    o = jnp.dot(pr.astype(BF16), v_ref[...], preferred_element_type=F32)
    o_ref[...] = (o / den).astype(o_ref.dtype)


def ctx_dense_attention(p, row0, n_ctx, nb, heads, prev):
    xb0 = row0 // n_ctx
    return pl.pallas_call(
        _ctx_attn_kernel,
        grid=(nb, heads),
        in_specs=[pl.BlockSpec((n_ctx, HEAD_DIM), lambda b, h: (xb0 + b, h)),
                  pl.BlockSpec((n_ctx, HEAD_DIM), lambda b, h: (xb0 + b, heads + h)),
                  pl.BlockSpec((n_ctx, HEAD_DIM), lambda b, h: (xb0 + b, 2 * heads + h)),
                  pl.BlockSpec(memory_space=pl.ANY)],
        out_specs=pl.BlockSpec((n_ctx, HEAD_DIM), lambda b, h: (xb0 + b, h)),
        out_shape=jax.ShapeDtypeStruct(prev.shape, BF16),
        input_output_aliases={3: 0},
        compiler_params=_cp(("arbitrary", "arbitrary")), name="ctx_dense_attention",
    )(p, p, p, prev)


MOE_TM = 256


TOK_BITS = 14


def _expert_kernel(te_ref, tv_ref, code_ref, h_hbm, wgu_ref, bgu_ref, wdn_ref, bdn_ref, y_hbm,
                   xbuf, ybuf, gsem, ssem, *, de, tm):
    i = pl.program_id(0)
    slot = i % 2
    other = 1 - slot
    xs0 = lax.rem(i, 3)
    xs1 = lax.rem(i + 1, 3)
    xs2 = lax.rem(i + 2, 3)
    dh = xbuf.shape[2]

    def gather_copy(tile, sl, r):
        tok = code_ref[(tile + 2) * tm + r] & ((1 << TOK_BITS) - 1)
        return pltpu.make_async_copy(h_hbm.at[pl.ds(tok, 1)], xbuf.at[sl, pl.ds(r, 1)], gsem.at[sl])

    def scatter_copy(tile, sl, r):
        dst = lax.shift_right_logical(code_ref[(tile + 2) * tm + r], TOK_BITS)
        return pltpu.make_async_copy(ybuf.at[sl, pl.ds(r, 1)], y_hbm.at[pl.ds(dst, 1)], ssem.at[sl])

    def start_all(copy_fn, tile, sl):
        def body(r, carry):
            copy_fn(tile, sl, r).start()
            return carry
        lax.fori_loop(0, tm, body, 0, unroll=8)

    def wait_all(buf, sem, sl):
        pltpu.make_async_copy(buf.at[sl], buf.at[sl], sem.at[sl]).wait()

    @pl.when(i == 0)
    def _():
        ybuf[...] = jnp.zeros_like(ybuf)
        start_all(gather_copy, 0, 0)
        start_all(gather_copy, 1, 1)
        start_all(scatter_copy, -2, 0)

    @pl.when(tv_ref[i] != 0)
    def _():
        wait_all(xbuf, gsem, xs0)
        kc = 256
        n_kc = dh // kc
        g_per = tm // n_kc
        acc = None
        for c in range(n_kc):
            for r in range(c * g_per, (c + 1) * g_per):
                gather_copy(i + 2, xs2, r).start(priority=r % 2)
            lo, hi = _unpack_halves(xbuf[xs0, :, c * kc:(c + 1) * kc])
            xk = jnp.concatenate([lo.astype(BF16), hi.astype(BF16)], axis=1)
            wk = jnp.concatenate([wgu_ref[0, 0, c * kc:(c + 1) * kc, :].astype(BF16),
                                  wgu_ref[0, 0, dh + c * kc:dh + (c + 1) * kc, :].astype(BF16)], axis=0)
            part = jnp.dot(xk, wk, preferred_element_type=F32)
            acc = part if acc is None else acc + part
        gu = acc + bgu_ref[0, 0]
        gate = jnp.minimum(gu[:, :de], SWIGLU_LIMIT)
        up = jnp.clip(gu[:, de:], -SWIGLU_LIMIT, SWIGLU_LIMIT)
        act = (gate * _sigmoid(SWIGLU_ALPHA * gate) * (up + 1.0)).astype(BF16)
        wait_all(ybuf, ssem, slot)
        yc = 512
        n_yc = dh // yc
        s_per = tm // n_yc
        for c in range(n_yc):
            for r in range(c * s_per, (c + 1) * s_per):
                scatter_copy(i - 1, other, r).start(priority=r % 2)
            cl = slice(c * yc, (c + 1) * yc)
            ch = slice(dh + c * yc, dh + (c + 1) * yc)
            y_lo = (jnp.dot(act, wdn_ref[0, 0, :, cl].astype(BF16), preferred_element_type=F32)
                    + bdn_ref[0, 0, :, cl])
            y_hi = (jnp.dot(act, wdn_ref[0, 0, :, ch].astype(BF16), preferred_element_type=F32)
                    + bdn_ref[0, 0, :, ch])
            ybuf[slot, :, cl] = _pack_pair(y_lo.astype(BF16).astype(F32), y_hi.astype(BF16).astype(F32))

    @pl.when(jnp.logical_and(tv_ref[i] == 0, jnp.logical_and(i > 0, tv_ref[jnp.maximum(i - 1, 0)] != 0)))
    def _():
        wait_all(xbuf, gsem, xs0)
        wait_all(xbuf, gsem, xs1)
        wait_all(ybuf, ssem, slot)
        start_all(scatter_copy, i - 1, other)
        wait_all(ybuf, ssem, other)


def expert_ffn(h_packed, code, tile_e, tile_valid, w_gu, b_gu, w_dn, b_dn, layer, n_slots):
    t, dh = h_packed.shape
    d = 2 * dh
    nl, ne, _, f2 = w_gu.shape
    de = f2 // 2
    tm = MOE_TM
    nt = tile_e.shape[0]
    assert code.shape[0] == (nt + 3) * tm and tm % (dh // 256) == 0 and dh % 512 == 0
    grid_spec = pltpu.PrefetchScalarGridSpec(
        num_scalar_prefetch=3,
        grid=(nt,),
        in_specs=[
            pl.BlockSpec(memory_space=pl.ANY),
            pl.BlockSpec((1, 1, d, f2), lambda i, te, tv, cd: (layer, te[i], 0, 0)),
            pl.BlockSpec((1, 1, 1, f2), lambda i, te, tv, cd: (layer, te[i], 0, 0)),
            pl.BlockSpec((1, 1, de, d), lambda i, te, tv, cd: (layer, te[i], 0, 0)),
            pl.BlockSpec((1, 1, 1, d), lambda i, te, tv, cd: (layer, te[i], 0, 0)),
        ],
        out_specs=pl.BlockSpec(memory_space=pl.ANY),
        scratch_shapes=[pltpu.VMEM((3, tm, dh), jnp.uint32), pltpu.VMEM((2, tm, dh), jnp.uint32),
                        pltpu.SemaphoreType.DMA((3,)), pltpu.SemaphoreType.DMA((2,))],
    )
    return pl.pallas_call(
        functools.partial(_expert_kernel, de=de, tm=tm),
        grid_spec=grid_spec,
        out_shape=jax.ShapeDtypeStruct((n_slots + tm, dh), jnp.uint32),
        compiler_params=_cp(("arbitrary",)), name="expert_ffn",
    )(tile_e, tile_valid, code, h_packed, w_gu, b_gu.reshape(nl, ne, 1, f2), w_dn, b_dn.reshape(nl, ne, 1, d))


def _combine_kernel(x_ref, y0_ref, y1_ref, y2_ref, y3_ref, wt_ref, g_ref, *rest, final_norm, next_mod):
    o_ref = rest[-2] if next_mod else rest[-1]
    wt = wt_ref[...]
    half = y0_ref.shape[1]
    acc_lo = jnp.zeros((x_ref.shape[0], half), F32)
    acc_hi = jnp.zeros((x_ref.shape[0], half), F32)
    for k, y_ref in enumerate((y0_ref, y1_ref, y2_ref, y3_ref)):
        lo, hi = _unpack_halves(y_ref[...])
        acc_lo = acc_lo + wt[:, k:k + 1] * lo
        acc_hi = acc_hi + wt[:, k:k + 1] * hi
    x_lo = x_ref[:, :half] + g_ref[0, :, :half] * acc_lo
    x_hi = x_ref[:, half:] + g_ref[0, :, half:] * acc_hi
    if final_norm:
        nf_ref = rest[0]
        ms = (jnp.sum(x_lo * x_lo, axis=-1, keepdims=True)
              + jnp.sum(x_hi * x_hi, axis=-1, keepdims=True)) / (2 * half)
        inv = lax.rsqrt(ms + EPS)
        x_lo = x_lo * inv * nf_ref[:, :half]
        x_hi = x_hi * inv * nf_ref[:, half:]
    o_ref[:, :half] = x_lo
    o_ref[:, half:] = x_hi
    if next_mod:
        gn_ref, scn_ref, shn_ref, h_ref = rest[0], rest[1], rest[2], rest[-1]
        ms = (jnp.sum(x_lo * x_lo, axis=-1, keepdims=True)
              + jnp.sum(x_hi * x_hi, axis=-1, keepdims=True)) / (2 * half)
        inv = lax.rsqrt(ms + EPS)
        for sl, xv in ((slice(0, half), x_lo), (slice(half, 2 * half), x_hi)):
            y = xv * inv * gn_ref[:, sl]
            h_ref[:, sl] = (y * (1.0 + scn_ref[0, :, sl]) + shn_ref[0, :, sl]).astype(h_ref.dtype)


def moe_combine(x, y4, wts, g3, seg, tm, norm_final=None, next_mod=None):
    t, d = x.shape
    tiles = t // tm
    in_specs = [pl.BlockSpec((tm, d), lambda i: (i, 0))]
    in_specs += [pl.BlockSpec((tm, d // 2), functools.partial(lambda i, k: (k * tiles + i, 0), k=k))
                 for k in range(TOP_K)]
    in_specs += [pl.BlockSpec((tm, LANES), lambda i: (i, 0)),
                 pl.BlockSpec((1, 1, d), lambda i: (seg(i), 0, 0))]
    args = [x, y4, y4, y4, y4, wts, g3]
    if norm_final is not None:
        in_specs.append(pl.BlockSpec((1, d), lambda i: (0, 0)))
        args.append(norm_final.reshape(1, d))
    out_specs = pl.BlockSpec((tm, d), lambda i: (i, 0))
    out_shape = jax.ShapeDtypeStruct((t, d), F32)
    if next_mod is not None:
        assert norm_final is None
        g_n, sc_n, sh_n = next_mod
        in_specs += [pl.BlockSpec((1, d), lambda i: (0, 0)),
                     pl.BlockSpec((1, 1, d), lambda i: (seg(i), 0, 0)),
                     pl.BlockSpec((1, 1, d), lambda i: (seg(i), 0, 0))]
        args += [g_n.reshape(1, d), sc_n, sh_n]
        out_specs = [out_specs, pl.BlockSpec((tm, d), lambda i: (i, 0))]
        out_shape = [out_shape, jax.ShapeDtypeStruct((t, d), BF16)]
    return pl.pallas_call(
        functools.partial(_combine_kernel, final_norm=norm_final is not None, next_mod=next_mod is not None),
        grid=(tiles,), in_specs=in_specs, out_specs=out_specs, out_shape=out_shape,
        compiler_params=_cp(("arbitrary",)), name="moe_combine",
    )(*args)


def moe_layer(x, g_ffn, sc3, sh3, g3, seg, tm, w_r, b_r, w_gu, b_gu, w_dn, b_dn, layer, norm_final=None,
              next_mod=None):
    t, d = x.shape
    h, topi, wts, rk, cnt = modulate(x, g_ffn, sc3, sh3, seg, tm, None, w_r, b_r)
    tme = MOE_TM
    ns = t * TOP_K
    nt = -(-(ns + N_EXPERTS * (tme - 1)) // tme) + 1
    counts = cnt[0, :N_EXPERTS]
    padded = ((counts + tme - 1) // tme) * tme
    e_i = jnp.arange(N_EXPERTS, dtype=jnp.int32)
    ends = jnp.sum(jnp.where(e_i[None, :] <= e_i[:, None], padded[None, :], 0), axis=1)
    offs = ends - padded
    ef = topi[:, :TOP_K]
    onehot = ef[:, :, None] == jnp.arange(N_EXPERTS, dtype=jnp.int32)[None, None, :]
    pos = rk[:, :TOP_K] + jnp.sum(jnp.where(onehot, offs[None, None, :], 0), axis=-1)
    assert t <= (1 << TOK_BITS) and ns + tme < (1 << (31 - TOK_BITS))
    tok = jnp.arange(t, dtype=jnp.int32)[:, None]
    slot_code = tok | ((jnp.arange(TOP_K, dtype=jnp.int32)[None, :] * t + tok) << TOK_BITS)
    pad_code = (ns + (jnp.arange((nt + 3) * tme, dtype=jnp.int32) % tme)) << TOK_BITS
    code = pad_code.at[pos.reshape(ns) + 2 * tme].set(slot_code.reshape(ns))
    tile_start = jnp.arange(nt, dtype=jnp.int32) * tme
    tile_valid = (tile_start < ends[-1]).astype(jnp.int32)
    tile_e = jnp.sum((ends[None, :] <= tile_start[:, None]).astype(jnp.int32), axis=1)
    tile_e = jnp.minimum(tile_e, N_EXPERTS - 1)
    tile_e = jnp.where(tile_valid != 0, tile_e, jnp.max(jnp.where(tile_valid != 0, tile_e, 0)))
    y4 = expert_ffn(h, code, tile_e, tile_valid, w_gu, b_gu, w_dn, b_dn, layer, ns)
    return moe_combine(x, y4, wts, g3, seg, tm, norm_final, next_mod)


def kernel(x, c, ctx, c_ctx, ada_w1, ada_w2, ada_b, norm_mix, norm_ffn, ev_w_in, ev_v_norm, ev_w_s, ev_b_s, ev_w_out, od_w_in, od_rpb, od_lam, od_sub_norm, od_w_out, moe_w_r, moe_b_r, moe_w_gu, moe_b_gu, moe_w_dn, moe_b_dn, norm_final):
    nb, seq, d = x.shape
    n_ctx = ctx.shape[1]
    depth = ada_w1.shape[0]
    n_lat = nb * seq
    t_all = n_lat + nb * n_ctx
    tm = 512 if (t_all % 512 == 0 and seq % 512 == 0) else 256
    tms = 256
    assert n_lat % tm == 0 and (nb * n_ctx) % tms == 0 and seq % tms == 0 and nb + 1 <= 8

    def seg_for(tile):
        return _seg_fn(n_lat // tile, seq // tile, nb)

    ev_w_out_b = cast_bf16(ev_w_out, 512)
    od_w_out_b = cast_bf16(od_w_out, 512)

    xs = jnp.concatenate([x.reshape(n_lat, d), ctx.reshape(nb * n_ctx, d)], axis=0)

    dft_consts = dft_constants(seq, n_ctx)
    cv = jnp.zeros((8, d), F32).at[:nb].set(c).at[nb].set(c_ctx)
    mods = ada_all(cv, ada_w1, ada_w2, ada_b)
    mod6 = [[mods[i].reshape(8, 6, d)[:, q, :].reshape(8, 1, d) for q in range(6)] for i in range(depth)]

    h = None
    for i in range(depth):
        last = i == depth - 1
        j = i // 2
        sh1, sc1, g1, sh2, sc2, g2 = mod6[i]
        if h is None:
            h = modulate(xs, norm_mix[i], sc1, sh1, seg_for(tms), tms)
        n_out = n_lat if last else t_all
        if i % 2 == 0:
            p = matmul(h, ev_w_in, j, tm)
            y1 = even_a(p, ev_v_norm[j], ev_w_s[j], ev_b_s[j], tms)
            y2 = even_b(p, dft_consts, seq, n_ctx, nb, tms)
            w_out = ev_w_out_b
        else:
            lambda_init = 0.8 - 0.6 * math.exp(-0.3 * i)
            p = matmul(h, od_w_in, j, tm)
            cw = p.shape[1] // 6
            heads_c = cw // HEAD_DIM
            heads_d = cw // (2 * HEAD_DIM)
            y1 = neighbourhood_attention(p, od_rpb[j], seq, n_ctx, nb, heads_c, n_out)
            qk = rope_qk(p, seq, nb, tms)
            nblk = cw // (2 * HEAD_DIM)
            seg_x = (p, n_lat, 4 * nblk, p, n_lat, 5 * nblk, n_ctx)
            seg_l = (qk, 0, nblk, p, 0, 5 * nblk, seq)
            y2 = diff_attention(qk, 0, 0, seq, [seg_x, seg_l], od_lam[j], od_sub_norm[j], nb, heads_d,
                                lambda_init, 512, n_out, 0)
            if not last:
                y1 = ctx_dense_attention(p, n_lat, n_ctx, nb, heads_c, y1)
                y2 = diff_attention(p, n_lat, 3 * nblk, n_ctx, [seg_x], od_lam[j], od_sub_norm[j], nb,
                                    heads_d, lambda_init, 256, n_out, n_lat, prev=y2)
            w_out = od_w_out_b
        xs = matmul_residual(y1, y2, w_out, j, xs, g1, seg_for(tm), tm, n_out)
        next_mod = None if last else (norm_mix[i + 1], mod6[i + 1][1], mod6[i + 1][0])
        res = moe_layer(xs, norm_ffn[i], sc2, sh2, g2, seg_for(tms), tms, moe_w_r[i], moe_b_r[i],
                        moe_w_gu, moe_b_gu, moe_w_dn, moe_b_dn, i, norm_final if last else None, next_mod)
        xs, h = (res, None) if last else res
    return xs.reshape(nb, seq, d)
```
